```python
import jax, jax.numpy as jnp
from jax import lax
import numpy as np

D_MODEL = 4096
BATCH = 2
SEQ = 4096
DEPTH = 2

N_MIXERS = 2
HEAD_DIM = 64
N_Q_HEADS = D_MODEL // HEAD_DIM
N_KV_HEADS = N_Q_HEADS // 8
GROUP = N_Q_HEADS // N_KV_HEADS
ATTN_INNER = N_Q_HEADS * HEAD_DIM
KV_DIM = N_KV_HEADS * HEAD_DIM
ATTN_IN_COLS = 2 * ATTN_INNER + 2 * KV_DIM
WINDOW = 128
BLOCK = 128
CONV_INNER = D_MODEL
CONV_WIDTH = 31
CONV_IN_COLS = 3 * CONV_INNER
N_ATTN_LAYERS = (DEPTH + N_MIXERS - 1) // N_MIXERS
N_CONV_LAYERS = DEPTH // N_MIXERS
EPS = 1e-5

kernel_name = "hybrid_swa_sink_conformer_conv"


def rms_norm(x, g):
    xf = x.astype(jnp.float32)
    xf = xf * lax.rsqrt(jnp.mean(xf * xf, axis=-1, keepdims=True) + EPS)
    return (xf * g.astype(jnp.float32)).astype(x.dtype)


def layer_norm(x, g, b):
    xf = x.astype(jnp.float32)
    mu = jnp.mean(xf, axis=-1, keepdims=True)
    var = jnp.mean(jnp.square(xf - mu), axis=-1, keepdims=True)
    y = (xf - mu) * lax.rsqrt(var + EPS) * g.astype(jnp.float32) + b.astype(jnp.float32)
    return y.astype(x.dtype)


def banded_sink_attention(q, k, v, sinks):
    B, S = q.shape[0], q.shape[1]
    nb = S // BLOCK
    qb = q.reshape(B, nb, BLOCK, N_KV_HEADS, GROUP, HEAD_DIM)
    kb = k.reshape(B, nb, BLOCK, N_KV_HEADS, HEAD_DIM)
    vb = v.reshape(B, nb, BLOCK, N_KV_HEADS, HEAD_DIM)
    kk = jnp.concatenate([jnp.concatenate([jnp.zeros_like(kb[:, :1]), kb[:, :-1]], axis=1), kb], axis=2)
    vv = jnp.concatenate([jnp.concatenate([jnp.zeros_like(vb[:, :1]), vb[:, :-1]], axis=1), vb], axis=2)
    scale = HEAD_DIM ** -0.5
    s = jnp.einsum('bnqkgd,bnskd->bnkgqs', qb, kk).astype(jnp.float32) * scale
    qi = jnp.arange(BLOCK)[:, None]
    kj = jnp.arange(2 * BLOCK)[None, :]
    diff = BLOCK + qi - kj
    band = (diff >= 0) & (diff < WINDOW)
    valid_prev = (jnp.arange(nb)[:, None, None] > 0) | (kj >= BLOCK)[None]
    mask = band[None] & valid_prev
    s = jnp.where(mask[None, :, None, None], s, -jnp.inf)
    sink = sinks.astype(jnp.float32).reshape(N_KV_HEADS, GROUP)[None, None, :, :, None, None]
    m = jnp.maximum(jnp.max(s, axis=-1, keepdims=True), sink)
    e = jnp.exp(s - m)
    p = e / (jnp.sum(e, axis=-1, keepdims=True) + jnp.exp(sink - m))
    o = jnp.einsum('bnkgqs,bnskd->bnqkgd', p.astype(v.dtype), vv)
    return o.reshape(B, S, ATTN_INNER)


def attention_mixer(h, w_in, b_in, sinks, w_out, b_out):
    B, S, _ = h.shape
    proj = h @ w_in + b_in
    q, k, v, z = jnp.split(proj, [ATTN_INNER, ATTN_INNER + KV_DIM, ATTN_INNER + 2 * KV_DIM], axis=-1)
    q = q.reshape(B, S, N_Q_HEADS, HEAD_DIM)
    k = k.reshape(B, S, N_KV_HEADS, HEAD_DIM)
    v = v.reshape(B, S, N_KV_HEADS, HEAD_DIM)
    o = banded_sink_attention(q, k, v, sinks)
    return (o * jax.nn.silu(z)) @ w_out + b_out


def conformer_conv_mixer(h, w_in, b_in, dw_w, dw_b, ln_g, ln_b, w_out, b_out):
    proj = h @ w_in + b_in
    a, g, z = jnp.split(proj, 3, axis=-1)
    u = a * jax.nn.sigmoid(g)
    u = lax.conv_general_dilated(
        u, dw_w[:, None, :].astype(u.dtype), window_strides=(1,),
        padding=[(CONV_WIDTH - 1, 0)], dimension_numbers=('NWC', 'WIO', 'NWC'),
        feature_group_count=CONV_INNER) + dw_b
    u = jax.nn.silu(layer_norm(u, ln_g, ln_b))
    return (u * jax.nn.silu(z)) @ w_out + b_out


def setup_inputs(seed: int = 0) -> dict:
    key = jax.random.key(seed)
    ks = jax.random.split(key, 18)
    nrm = jax.random.normal
    NA, NC = N_ATTN_LAYERS, N_CONV_LAYERS
    return {
        "x": nrm(ks[0], (BATCH, SEQ, D_MODEL), jnp.float32),
        "norm_g": 1.0 + 0.01 * nrm(ks[1], (DEPTH, D_MODEL), jnp.float32),
        "attn_w_in": nrm(ks[2], (NA, D_MODEL, ATTN_IN_COLS), jnp.float32) * D_MODEL ** -0.5,
        "attn_b_in": 0.01 * nrm(ks[3], (NA, ATTN_IN_COLS), jnp.float32),
        "attn_sinks": 0.5 * nrm(ks[4], (NA, N_Q_HEADS), jnp.float32),
        "attn_w_out": nrm(ks[5], (NA, ATTN_INNER, D_MODEL), jnp.float32) * ATTN_INNER ** -0.5,
        "attn_b_out": 0.01 * nrm(ks[6], (NA, D_MODEL), jnp.float32),
        "conv_w_in": nrm(ks[7], (NC, D_MODEL, CONV_IN_COLS), jnp.float32) * D_MODEL ** -0.5,
        "conv_b_in": 0.01 * nrm(ks[8], (NC, CONV_IN_COLS), jnp.float32),
        "conv_dw_w": nrm(ks[9], (NC, CONV_WIDTH, CONV_INNER), jnp.float32) * CONV_WIDTH ** -0.5,
        "conv_dw_b": 0.01 * nrm(ks[10], (NC, CONV_INNER), jnp.float32),
        "conv_ln_g": 1.0 + 0.01 * nrm(ks[11], (NC, CONV_INNER), jnp.float32),
        "conv_ln_b": 0.01 * nrm(ks[12], (NC, CONV_INNER), jnp.float32),
        "conv_w_out": nrm(ks[13], (NC, CONV_INNER, D_MODEL), jnp.float32) * CONV_INNER ** -0.5,
        "conv_b_out": 0.01 * nrm(ks[14], (NC, D_MODEL), jnp.float32),
        "final_g": 1.0 + 0.01 * nrm(ks[15], (D_MODEL,), jnp.float32),
    }


def reference(x, norm_g, attn_w_in, attn_b_in, attn_sinks, attn_w_out, attn_b_out,
              conv_w_in, conv_b_in, conv_dw_w, conv_dw_b, conv_ln_g, conv_ln_b,
              conv_w_out, conv_b_out, final_g):
    for i in range(DEPTH):
        h = rms_norm(x, norm_g[i])
        j = i // N_MIXERS
        if i % N_MIXERS == 0:
            x = x + attention_mixer(h, attn_w_in[j], attn_b_in[j], attn_sinks[j],
                                    attn_w_out[j], attn_b_out[j])
        else:
            x = x + conformer_conv_mixer(h, conv_w_in[j], conv_b_in[j], conv_dw_w[j], conv_dw_b[j],
                                         conv_ln_g[j], conv_ln_b[j], conv_w_out[j], conv_b_out[j])
    return rms_norm(x, final_g)
```

```python
import functools

import jax
import jax.numpy as jnp
from jax import lax
from jax.experimental import pallas as pl
from jax.experimental.pallas import tpu as pltpu

F32 = jnp.float32
BF16 = jnp.bfloat16

D_MODEL = 4096
HEAD_DIM = 64
N_Q_HEADS = D_MODEL // HEAD_DIM
N_KV_HEADS = N_Q_HEADS // 8
GROUP = N_Q_HEADS // N_KV_HEADS
ATTN_INNER = N_Q_HEADS * HEAD_DIM
KV_DIM = N_KV_HEADS * HEAD_DIM
WINDOW = 128
BLOCK = 128
CONV_INNER = D_MODEL
CONV_WIDTH = 31
EPS = 1e-5

VMEM_LIMIT_BYTES = 56 * 1024 * 1024
LANES = 128

MM_TM = 1024
MM_TN = 512

NORM_ROWS = 256
CONV_ROWS = 256
CONV_HALO = 32
CONV_ROW_GROUP = 32
HEADS_PER_DOT = 4
NEG_BIG = -1e30


def _params(*sem):
    return pltpu.CompilerParams(dimension_semantics=sem,
                                vmem_limit_bytes=VMEM_LIMIT_BYTES)


def _rmsnorm_kernel(x_ref, g_ref, o_ref):
    x = x_ref[...]
    r = lax.rsqrt(jnp.mean(x * x, axis=-1, keepdims=True) + EPS)
    o_ref[...] = (x * r * g_ref[...]).astype(o_ref.dtype)


def _rmsnorm(x, g, out_dtype):
    m, d = x.shape
    return pl.pallas_call(
        _rmsnorm_kernel,
        grid=(m // NORM_ROWS,),
        in_specs=[pl.BlockSpec((NORM_ROWS, d), lambda i: (i, 0)),
                  pl.BlockSpec((1, d), lambda i: (0, 0))],
        out_specs=pl.BlockSpec((NORM_ROWS, d), lambda i: (i, 0)),
        out_shape=jax.ShapeDtypeStruct((m, d), out_dtype),
        compiler_params=_params("parallel"),
        name="rmsnorm",
    )(x, g.reshape(1, d))


def _matmul_kernel(*refs, has_resid):
    if has_resid:
        lhs_ref, w_ref, b_ref, r_ref, o_ref, wbf_ref = refs
    else:
        lhs_ref, w_ref, b_ref, o_ref, wbf_ref = refs

    @pl.when(pl.program_id(1) == 0)
    def _():
        wbf_ref[...] = w_ref[...].astype(BF16)

    acc = jnp.dot(lhs_ref[...], wbf_ref[...], preferred_element_type=F32)
    acc = acc + b_ref[...]
    if has_resid:
        acc = acc + r_ref[...]
    o_ref[...] = acc.astype(o_ref.dtype)


def _matmul(lhs, w3, b, out_dtype, *, resid=None, out_col_block=None, name):
    m, k = lhs.shape
    n = w3.shape[-1]
    nj, ni = n // MM_TN, m // MM_TM
    if out_col_block is None:
        out_col_block = lambda j: j
    in_specs = [
        pl.BlockSpec((MM_TM, k), lambda j, i: (i, 0)),
        pl.BlockSpec((None, k, MM_TN), lambda j, i: (0, 0, j)),
        pl.BlockSpec((1, MM_TN), lambda j, i: (0, j)),
    ]
    args = [lhs, w3, b.reshape(1, n)]
    if resid is not None:
        in_specs.append(pl.BlockSpec((MM_TM, MM_TN), lambda j, i: (i, j)))
        args.append(resid)
    return pl.pallas_call(
        functools.partial(_matmul_kernel, has_resid=resid is not None),
        grid=(nj, ni),
        in_specs=in_specs,
        out_specs=pl.BlockSpec((MM_TM, MM_TN),
                               lambda j, i: (i, out_col_block(j))),
        out_shape=jax.ShapeDtypeStruct((m, n), out_dtype),
        scratch_shapes=[pltpu.VMEM((k, MM_TN), BF16)],
        compiler_params=_params("arbitrary", "arbitrary"),
        name=name,
    )(*args)


def _attn_kernel(sink_ref, q_ref, z_ref, k_ref, v_ref, kp_ref, vp_ref, o_ref,
                 *, blocks_per_seq):
    n = pl.program_id(0)
    first_key = jnp.where((n % blocks_per_seq) != 0, 0, BLOCK)

    width = 2 * BLOCK
    qi = lax.broadcasted_iota(jnp.int32, (BLOCK, width), 0)
    kj = lax.broadcasted_iota(jnp.int32, (BLOCK, width), 1)
    valid = (kj > qi) & (kj <= qi + WINDOW) & (kj >= first_key)
    bias = jnp.where(valid, 0.0, NEG_BIG).astype(F32)

    dot_w = HEADS_PER_DOT * HEAD_DIM
    lane128 = lax.broadcasted_iota(jnp.int32, (width, LANES), 1)
    bd_lane_group = lax.broadcasted_iota(jnp.int32, (width, dot_w), 1) // HEAD_DIM
    out_lane_group = lax.broadcasted_iota(jnp.int32, (BLOCK, dot_w), 1) // HEAD_DIM

    def block_diag(pair_f32, odd, scale):
        rolled = pltpu.roll(pair_f32, HEAD_DIM, axis=1)
        if odd:
            rep = jnp.where(lane128 < HEAD_DIM, rolled, pair_f32)
        else:
            rep = jnp.where(lane128 < HEAD_DIM, pair_f32, rolled)
        rep = (rep * scale).astype(BF16)
        rep = jnp.concatenate([rep, rep], axis=1)
        zero = jnp.zeros_like(rep)
        return jnp.concatenate(
            [jnp.where(bd_lane_group == h, rep, zero) for h in range(HEADS_PER_DOT)],
            axis=0)

    scale = HEAD_DIM ** -0.5
    for pair in range(N_KV_HEADS // 2):
        lanes = slice(pair * LANES, (pair + 1) * LANES)
        k_pair = jnp.concatenate([kp_ref[:, lanes], k_ref[:, lanes]], axis=0).astype(F32)
        v_pair = jnp.concatenate([vp_ref[:, lanes], v_ref[:, lanes]], axis=0).astype(F32)
        for odd in range(2):
            kvh = 2 * pair + odd
            kbd_t = block_diag(k_pair, odd, scale)
            vbd = block_diag(v_pair, odd, 1.0)
            for half in range(GROUP // HEADS_PER_DOT):
                head0 = kvh * GROUP + half * HEADS_PER_DOT
                cols = slice(head0 * HEAD_DIM, head0 * HEAD_DIM + dot_w)
                s = lax.dot_general(q_ref[:, cols], kbd_t,
                                    (((1,), (1,)), ((), ())),
                                    preferred_element_type=F32)
                probs = []
                inv_l = []
                for h in range(HEADS_PER_DOT):
                    sink = sink_ref[head0 + h]
                    sh = s[:, h * width:(h + 1) * width] + bias
                    m = jnp.maximum(jnp.max(sh, axis=-1, keepdims=True), sink)
                    e = jnp.exp(sh - m)
                    l = jnp.sum(e, axis=-1, keepdims=True) + jnp.exp(sink - m)
                    probs.append(e.astype(BF16))
                    inv_l.append(1.0 / l)
                p = jnp.concatenate(probs, axis=1)
                o = jnp.dot(p, vbd, preferred_element_type=F32)
                inv = inv_l[HEADS_PER_DOT - 1]
                for h in range(HEADS_PER_DOT - 2, -1, -1):
                    inv = jnp.where(out_lane_group == h, inv_l[h], inv)
                zt = z_ref[:, cols].astype(F32)
                gate = zt / (1.0 + jnp.exp(-zt))
                o_ref[:, cols] = (o * inv * gate).astype(o_ref.dtype)


def _attention(proj, sinks, seq):
    m = proj.shape[0]
    nblk = m // BLOCK
    bps = seq // BLOCK
    kcol = 2 * ATTN_INNER // KV_DIM
    prev = lambda n: jnp.where(n % bps == 0, n, n - 1)
    return pl.pallas_call(
        functools.partial(_attn_kernel, blocks_per_seq=bps),
        grid=(nblk,),
        in_specs=[
            pl.BlockSpec(memory_space=pltpu.SMEM),
            pl.BlockSpec((BLOCK, ATTN_INNER), lambda n: (n, 0)),
            pl.BlockSpec((BLOCK, ATTN_INNER), lambda n: (n, 1)),
            pl.BlockSpec((BLOCK, KV_DIM), lambda n: (n, kcol)),
            pl.BlockSpec((BLOCK, KV_DIM), lambda n: (n, kcol + 1)),
            pl.BlockSpec((BLOCK, KV_DIM), lambda n: (prev(n), kcol)),
            pl.BlockSpec((BLOCK, KV_DIM), lambda n: (prev(n), kcol + 1)),
        ],
        out_specs=pl.BlockSpec((BLOCK, ATTN_INNER), lambda n: (n, 0)),
        out_shape=jax.ShapeDtypeStruct((m, ATTN_INNER), BF16),
        compiler_params=_params("parallel"),
        name="swa_attention",
    )(sinks, proj, proj, proj, proj, proj, proj)


def _conv_kernel(a_ref, g_ref, z_ref, ah_ref, gh_ref, dw_ref, dwb_ref,
                 lng_ref, lnb_ref, o_ref, u_ref, c_ref):
    t = pl.program_id(1)

    a = a_ref[...].astype(F32)
    g = g_ref[...].astype(F32)
    u_ref[CONV_HALO:, :] = a / (1.0 + jnp.exp(-g))

    @pl.when(t == 0)
    def _():
        u_ref[:CONV_HALO, :] = jnp.zeros((CONV_HALO, CONV_INNER), F32)

    @pl.when(t != 0)
    def _():
        ah = ah_ref[...].astype(F32)
        gh = gh_ref[...].astype(F32)
        u_ref[:CONV_HALO, :] = ah / (1.0 + jnp.exp(-gh))

    first = CONV_HALO - (CONV_WIDTH - 1)

    def chunk_body(c, carry):
        cols = pl.ds(pl.multiple_of(c * LANES, LANES), LANES)
        w = dw_ref[:, cols]
        b = dwb_ref[:, cols]

        for r0 in range(0, CONV_ROWS, CONV_ROW_GROUP):
            acc = jnp.broadcast_to(b, (CONV_ROW_GROUP, LANES))
            for k in range(CONV_WIDTH):
                acc = acc + w[k:k + 1, :] * u_ref[pl.ds(r0 + first + k, CONV_ROW_GROUP), cols]
            c_ref[pl.ds(r0, CONV_ROW_GROUP), cols] = acc
        return carry

    lax.fori_loop(0, CONV_INNER // LANES, chunk_body, 0)

    c = c_ref[...]
    mu = jnp.mean(c, axis=-1, keepdims=True)
    d = c - mu
    var = jnp.mean(d * d, axis=-1, keepdims=True)
    y = d * lax.rsqrt(var + EPS) * lng_ref[...] + lnb_ref[...]
    y = y / (1.0 + jnp.exp(-y))
    z = z_ref[...].astype(F32)
    o_ref[...] = (y * (z / (1.0 + jnp.exp(-z)))).astype(o_ref.dtype)


def _conv_module(proj, dw_w, dw_b, ln_g, ln_b, batch, seq):
    m = proj.shape[0]
    c = CONV_INNER
    tiles = seq // CONV_ROWS
    ratio = CONV_ROWS // CONV_HALO
    row = lambda b, t: b * tiles + t
    halo_row = lambda b, t: jnp.maximum(row(b, t) * ratio - 1, 0)
    vec = lambda a: a.reshape(1, c)
    const = lambda shape: pl.BlockSpec(shape, lambda b, t: (0, 0))
    return pl.pallas_call(
        _conv_kernel,
        grid=(batch, tiles),
        in_specs=[
            pl.BlockSpec((CONV_ROWS, c), lambda b, t: (row(b, t), 0)),
            pl.BlockSpec((CONV_ROWS, c), lambda b, t: (row(b, t), 1)),
            pl.BlockSpec((CONV_ROWS, c), lambda b, t: (row(b, t), 2)),
            pl.BlockSpec((CONV_HALO, c), lambda b, t: (halo_row(b, t), 0)),
            pl.BlockSpec((CONV_HALO, c), lambda b, t: (halo_row(b, t), 1)),
            const((CONV_WIDTH, c)), const((1, c)), const((1, c)), const((1, c)),
        ],
        out_specs=pl.BlockSpec((CONV_ROWS, c), lambda b, t: (row(b, t), 0)),
        out_shape=jax.ShapeDtypeStruct((m, c), BF16),
        scratch_shapes=[pltpu.VMEM((CONV_HALO + CONV_ROWS, c), F32),
                        pltpu.VMEM((CONV_ROWS, c), F32)],
        compiler_params=_params("parallel", "arbitrary"),
        name="conv_module",
    )(proj, proj, proj, proj, proj, dw_w, vec(dw_b), vec(ln_g), vec(ln_b))


def _attn_out_col_block(j):
    q_tiles = ATTN_INNER // MM_TN
    kv_tiles = 2 * KV_DIM // MM_TN
    return jnp.where(j < q_tiles, j,
                     jnp.where(j < q_tiles + kv_tiles, j + q_tiles, j - kv_tiles))


def kernel(x, norm_g, attn_w_in, attn_b_in, attn_sinks, attn_w_out, attn_b_out,
           conv_w_in, conv_b_in, conv_dw_w, conv_dw_b, conv_ln_g, conv_ln_b,
           conv_w_out, conv_b_out, final_g):
    batch, seq, d = x.shape
    m = batch * seq
    x0 = x.reshape(m, d)

    h = _rmsnorm(x0, norm_g[0], BF16)
    proj = _matmul(h, attn_w_in, attn_b_in[0], BF16,
                   out_col_block=_attn_out_col_block, name="attn_in_proj")
    y = _attention(proj, attn_sinks[0], seq)
    x1 = _matmul(y, attn_w_out, attn_b_out[0], F32, resid=x0, name="attn_out_proj")

    h = _rmsnorm(x1, norm_g[1], BF16)
    proj = _matmul(h, conv_w_in, conv_b_in[0], BF16, name="conv_in_proj")
    y = _conv_module(proj, conv_dw_w[0], conv_dw_b[0], conv_ln_g[0], conv_ln_b[0],
                     batch, seq)
    x2 = _matmul(y, conv_w_out, conv_b_out[0], F32, resid=x1, name="conv_out_proj")

    return _rmsnorm(x2, final_g, F32).reshape(batch, seq, d)
```

```python
import functools

import jax
import jax.numpy as jnp
from jax import lax
from jax.experimental import pallas as pl
from jax.experimental.pallas import tpu as pltpu

F32 = jnp.float32
BF16 = jnp.bfloat16

D_MODEL = 4096
HEAD_DIM = 64
N_Q_HEADS = D_MODEL // HEAD_DIM
N_KV_HEADS = N_Q_HEADS // 8
GROUP = N_Q_HEADS // N_KV_HEADS
ATTN_INNER = N_Q_HEADS * HEAD_DIM
KV_DIM = N_KV_HEADS * HEAD_DIM
WINDOW = 128
BLOCK = 128
CONV_INNER = D_MODEL
CONV_WIDTH = 31
EPS = 1e-5

VMEM_LIMIT_BYTES = 56 * 1024 * 1024
LANES = 128
SUBLANES = 8

MM_TM = 1024
MM_TN = 512

NORM_ROWS = 256
CONV_HALO = 32
CONV_CHUNK = 256
CONV_ROW_GROUP = 32
GLU_PIECE_COLS = 256
GLU_K_SLICE = 512
HEADS_PER_DOT = 4
NEG_BIG = -1e30


def _params(*sem, flags=None):
    return pltpu.CompilerParams(dimension_semantics=sem,
                                vmem_limit_bytes=VMEM_LIMIT_BYTES, flags=flags)


def _sigmoid(x):
    return 0.5 * jnp.tanh(0.5 * x) + 0.5


def _silu(x):
    return x * _sigmoid(x)


def _rmsnorm_kernel(x_ref, g_ref, o_ref):
    x = x_ref[...]
    r = lax.rsqrt(jnp.mean(x * x, axis=-1, keepdims=True) + EPS)
    o_ref[...] = (x * r * g_ref[...]).astype(o_ref.dtype)


def _rmsnorm(x, g, out_dtype):
    m, d = x.shape
    return pl.pallas_call(
        _rmsnorm_kernel,
        grid=(m // NORM_ROWS,),
        in_specs=[pl.BlockSpec((NORM_ROWS, d), lambda i: (i, 0)),
                  pl.BlockSpec((1, d), lambda i: (0, 0))],
        out_specs=pl.BlockSpec((NORM_ROWS, d), lambda i: (i, 0)),
        out_shape=jax.ShapeDtypeStruct((m, d), out_dtype),
        compiler_params=_params("parallel"),
        name="rmsnorm",
    )(x, g.reshape(1, d))


def _matmul_kernel(*refs, has_resid, silu_from_tile):
    if has_resid:
        lhs_ref, w_ref, b_ref, r_ref, o_ref, wbf_ref = refs
    else:
        lhs_ref, w_ref, b_ref, o_ref, wbf_ref = refs

    @pl.when(pl.program_id(1) == 0)
    def _():
        wbf_ref[...] = w_ref[...].astype(BF16)

    acc = jnp.dot(lhs_ref[...], wbf_ref[...], preferred_element_type=F32)
    acc = acc + b_ref[...]
    if silu_from_tile is not None:
        acc = jnp.where(pl.program_id(0) >= silu_from_tile, _silu(acc), acc)
    if has_resid:
        acc = acc + r_ref[...]
    o_ref[...] = acc.astype(o_ref.dtype)


def _matmul(lhs, w3, b2, out_dtype, *, n_tiles, w_tile0=0, resid=None,
            out_col_block=None, silu_from_tile=None, name):
    m, k = lhs.shape
    ni = m // MM_TM
    if out_col_block is None:
        out_col_block = lambda j: j
    in_specs = [
        pl.BlockSpec((MM_TM, k), lambda j, i: (i, 0)),
        pl.BlockSpec((None, k, MM_TN), lambda j, i: (0, 0, j + w_tile0)),
        pl.BlockSpec((1, MM_TN), lambda j, i: (0, j + w_tile0)),
    ]
    args = [lhs, w3, b2]
    if resid is not None:
        in_specs.append(pl.BlockSpec((MM_TM, MM_TN), lambda j, i: (i, j)))
        args.append(resid)
    return pl.pallas_call(
        functools.partial(_matmul_kernel, has_resid=resid is not None,
                          silu_from_tile=silu_from_tile),
        grid=(n_tiles, ni),
        in_specs=in_specs,
        out_specs=pl.BlockSpec((MM_TM, MM_TN),
                               lambda j, i: (i, out_col_block(j))),
        out_shape=jax.ShapeDtypeStruct((m, n_tiles * MM_TN), out_dtype),
        scratch_shapes=[pltpu.VMEM((k, MM_TN), BF16)],
        compiler_params=_params("arbitrary", "arbitrary"),
        name=name,
    )(*args)


def _attn_kernel(sink_ref, q_ref, gate_ref, k_ref, v_ref, kp_ref, vp_ref, o_ref,
                 *, blocks_per_seq):
    n = pl.program_id(0)
    first_key = jnp.where((n % blocks_per_seq) != 0, 0, BLOCK)

    width = 2 * BLOCK
    qi = lax.broadcasted_iota(jnp.int32, (BLOCK, width), 0)
    kj = lax.broadcasted_iota(jnp.int32, (BLOCK, width), 1)
    valid = (kj > qi) & (kj <= qi + WINDOW) & (kj >= first_key)
    bias = jnp.where(valid, 0.0, NEG_BIG).astype(F32)

    dot_w = HEADS_PER_DOT * HEAD_DIM
    lane128 = lax.broadcasted_iota(jnp.int32, (width, LANES), 1)
    bd_lane_group = lax.broadcasted_iota(jnp.int32, (width, dot_w), 1) // HEAD_DIM
    out_lane_group = lax.broadcasted_iota(jnp.int32, (BLOCK, dot_w), 1) // HEAD_DIM

    def block_diag(pair_f32, odd, scale):
        rolled = pltpu.roll(pair_f32, HEAD_DIM, axis=1)
        if odd:
            rep = jnp.where(lane128 < HEAD_DIM, rolled, pair_f32)
        else:
            rep = jnp.where(lane128 < HEAD_DIM, pair_f32, rolled)
        rep = (rep * scale).astype(BF16)
        rep = jnp.concatenate([rep, rep], axis=1)
        zero = jnp.zeros_like(rep)
        return jnp.concatenate(
            [jnp.where(bd_lane_group == h, rep, zero) for h in range(HEADS_PER_DOT)],
            axis=0)

    scale = HEAD_DIM ** -0.5
    for pair in range(N_KV_HEADS // 2):
        lanes = slice(pair * LANES, (pair + 1) * LANES)
        k_pair = jnp.concatenate([kp_ref[:, lanes], k_ref[:, lanes]], axis=0).astype(F32)
        v_pair = jnp.concatenate([vp_ref[:, lanes], v_ref[:, lanes]], axis=0).astype(F32)
        for odd in range(2):
            kvh = 2 * pair + odd
            kbd_t = block_diag(k_pair, odd, scale)
            vbd = block_diag(v_pair, odd, 1.0)
            for half in range(GROUP // HEADS_PER_DOT):
                head0 = kvh * GROUP + half * HEADS_PER_DOT
                cols = slice(head0 * HEAD_DIM, head0 * HEAD_DIM + dot_w)
                s = lax.dot_general(q_ref[:, cols], kbd_t,
                                    (((1,), (1,)), ((), ())),
                                    preferred_element_type=F32)
                probs = []
                inv_l = []
                for h in range(HEADS_PER_DOT):
                    sink = sink_ref[head0 + h]
                    sh = s[:, h * width:(h + 1) * width] + bias
                    m = jnp.maximum(jnp.max(sh, axis=-1, keepdims=True), sink)
                    e = jnp.exp(sh - m)
                    l = jnp.sum(e, axis=-1, keepdims=True) + jnp.exp(sink - m)
                    probs.append(e.astype(BF16))
                    inv_l.append(1.0 / l)
                p = jnp.concatenate(probs, axis=1)
                o = jnp.dot(p, vbd, preferred_element_type=F32)
                inv = inv_l[HEADS_PER_DOT - 1]
                for h in range(HEADS_PER_DOT - 2, -1, -1):
                    inv = jnp.where(out_lane_group == h, inv_l[h], inv)
                gate = gate_ref[:, cols].astype(F32)
                o_ref[:, cols] = (o * inv * gate).astype(o_ref.dtype)


def _attention(proj, sinks, seq):
    m = proj.shape[0]
    nblk = m // BLOCK
    bps = seq // BLOCK
    kcol = 2 * ATTN_INNER // KV_DIM
    prev = lambda n: jnp.where(n % bps == 0, n, n - 1)
    return pl.pallas_call(
        functools.partial(_attn_kernel, blocks_per_seq=bps),
        grid=(nblk,),
        in_specs=[
            pl.BlockSpec(memory_space=pltpu.SMEM),
            pl.BlockSpec((BLOCK, ATTN_INNER), lambda n: (n, 0)),
            pl.BlockSpec((BLOCK, ATTN_INNER), lambda n: (n, 1)),
            pl.BlockSpec((BLOCK, KV_DIM), lambda n: (n, kcol)),
            pl.BlockSpec((BLOCK, KV_DIM), lambda n: (n, kcol + 1)),
            pl.BlockSpec((BLOCK, KV_DIM), lambda n: (prev(n), kcol)),
            pl.BlockSpec((BLOCK, KV_DIM), lambda n: (prev(n), kcol + 1)),
        ],
        out_specs=pl.BlockSpec((BLOCK, ATTN_INNER), lambda n: (n, 0)),
        out_shape=jax.ShapeDtypeStruct((m, ATTN_INNER), BF16),
        compiler_params=_params("parallel"),
        name="swa_attention",
    )(sinks, proj, proj, proj, proj, proj, proj)


def _shift_up_one_row(x, keep_mask):
    r = pltpu.roll(x, SUBLANES - 1, axis=1)
    nxt = jnp.concatenate([r[1:], r[-1:]], axis=0)
    return jnp.where(keep_mask, r, nxt)


def _causal_conv_group(u_ref, wb_ref, row0, lanes, keep_mask):
    nv = CONV_ROW_GROUP // SUBLANES
    off0 = CONV_HALO - (CONV_WIDTH - 1)

    def window(row, n):
        return u_ref[pl.ds(row, n * SUBLANES), lanes].reshape(n, SUBLANES, LANES)

    def phase_sum(s, n):
        acc = None
        for q in range((off0 + CONV_WIDTH - 1) // SUBLANES + 1):
            k = q * SUBLANES + s - off0
            if 0 <= k < CONV_WIDTH:
                term = wb_ref[k, :, lanes][None] * window(row0 + q * SUBLANES, n)
                acc = term if acc is None else acc + term
        return acc

    partial = phase_sum(SUBLANES - 1, nv + 1)
    for s in range(SUBLANES - 2, 0, -1):
        partial = phase_sum(s, nv + 1) + _shift_up_one_row(partial, keep_mask)
    out = phase_sum(0, nv) + _shift_up_one_row(partial, keep_mask)[:nv]
    return out.reshape(CONV_ROW_GROUP, LANES)


def _glu_conv_kernel(h_ref, wa_ref, wg_ref, ba_ref, bg_ref, dw_ref, dwb_ref,
                     c_ref, wa_bf, wg_bf, wb_ref, halo_ref, *u_refs, tiles_per_seq):
    i = pl.program_id(1)
    tm, tn = c_ref.shape

    @pl.when(i == 0)
    def _():
        wa_bf[...] = wa_ref[...].astype(BF16)
        wg_bf[...] = wg_ref[...].astype(BF16)
        for k in range(CONV_WIDTH):
            wb_ref[k] = jnp.broadcast_to(dw_ref[k:k + 1, :], (SUBLANES, tn))

    @pl.when(i % tiles_per_seq == 0)
    def _():
        halo_ref[...] = jnp.zeros((CONV_HALO, tn), F32)

    keep_mask = lax.broadcasted_iota(
        jnp.int32, (CONV_ROW_GROUP // SUBLANES + 1, SUBLANES, LANES), 1) < SUBLANES - 1

    n_chunks = len(u_refs)
    n_pieces = tn // GLU_PIECE_COLS

    kdim = h_ref.shape[1]

    def glu_steps(c, p):
        rows = slice(c * CONV_CHUNK, (c + 1) * CONV_CHUNK)
        cols = slice(p * GLU_PIECE_COLS, (p + 1) * GLU_PIECE_COLS)
        acc = {}

        def step(w_bf, name, k0):
            ks = slice(k0, k0 + GLU_K_SLICE)
            part = jnp.dot(h_ref[rows, ks], w_bf[ks, cols], preferred_element_type=F32)
            acc[name] = part if name not in acc else acc[name] + part

        def finish():
            a = acc["a"] + ba_ref[:, cols]
            g = acc["g"] + bg_ref[:, cols]
            u_refs[c][CONV_HALO:, cols] = a * _sigmoid(g)

        steps = []
        for k0 in range(0, kdim, GLU_K_SLICE):
            steps.append(functools.partial(step, wa_bf, "a", k0))
            steps.append(functools.partial(step, wg_bf, "g", k0))
        return steps, finish

    def glu_piece(c, p):
        steps, finish = glu_steps(c, p)
        for s in steps:
            s()
        finish()

    def fill_halo(c):
        src = halo_ref[...] if c == 0 else u_refs[c - 1][CONV_CHUNK:, :]
        u_refs[c][:CONV_HALO, :] = src

    def conv_groups(c):
        for lc in range(tn // LANES):
            lanes = slice(lc * LANES, (lc + 1) * LANES)
            for r0 in range(0, CONV_CHUNK, CONV_ROW_GROUP):
                def run(lanes=lanes, r0=r0):
                    out = _causal_conv_group(u_refs[c], wb_ref, r0, lanes, keep_mask)
                    o0 = c * CONV_CHUNK + r0
                    c_ref[o0:o0 + CONV_ROW_GROUP, lanes] = out + dwb_ref[:, lanes]
                yield run

    for p in range(n_pieces):
        glu_piece(0, p)
    for c in range(n_chunks):
        fill_halo(c)
        groups = list(conv_groups(c))
        per_piece = len(groups) // n_pieces
        for p in range(n_pieces):
            mine = groups[p * per_piece:(p + 1) * per_piece]
            if c + 1 < n_chunks:
                steps, finish = glu_steps(c + 1, p)
                per_group = len(steps) // len(mine)
                for gi, run in enumerate(mine):
                    for s in steps[gi * per_group:(gi + 1) * per_group]:
                        s()
                    run()
                finish()
            else:
                for run in mine:
                    run()

    halo_ref[...] = u_refs[-1][CONV_CHUNK:, :]


def _glu_conv(h, w3, b2, dw_w, dw_b, seq):
    m, k = h.shape
    c = CONV_INNER
    nj, ni = c // MM_TN, m // MM_TM
    once = pl.Buffered(1)
    return pl.pallas_call(
        functools.partial(_glu_conv_kernel, tiles_per_seq=seq // MM_TM),
        grid=(nj, ni),
        in_specs=[
            pl.BlockSpec((MM_TM, k), lambda j, i: (i, 0)),
            pl.BlockSpec((None, k, MM_TN), lambda j, i: (0, 0, j), pipeline_mode=once),
            pl.BlockSpec((None, k, MM_TN), lambda j, i: (0, 0, j + nj), pipeline_mode=once),
            pl.BlockSpec((1, MM_TN), lambda j, i: (0, j)),
            pl.BlockSpec((1, MM_TN), lambda j, i: (0, j + nj)),
            pl.BlockSpec((CONV_WIDTH, MM_TN), lambda j, i: (0, j)),
            pl.BlockSpec((1, MM_TN), lambda j, i: (0, j)),
        ],
        out_specs=pl.BlockSpec((MM_TM, MM_TN), lambda j, i: (i, j)),
        out_shape=jax.ShapeDtypeStruct((m, c), F32),
        scratch_shapes=[pltpu.VMEM((k, MM_TN), BF16),
                        pltpu.VMEM((k, MM_TN), BF16),
                        pltpu.VMEM((CONV_WIDTH, SUBLANES, MM_TN), F32),
                        pltpu.VMEM((CONV_HALO, MM_TN), F32)]
                       + [pltpu.VMEM((CONV_HALO + CONV_CHUNK, MM_TN), F32)] * (MM_TM // CONV_CHUNK),
        compiler_params=_params("arbitrary", "arbitrary"),
        name="glu_conv_in_proj",
    )(h, w3, w3, b2, b2, dw_w, dw_b.reshape(1, c))


def _ln_gate_kernel(c_ref, gate_ref, lng_ref, lnb_ref, o_ref):
    c = c_ref[...]
    mu = jnp.mean(c, axis=-1, keepdims=True)
    d = c - mu
    var = jnp.mean(d * d, axis=-1, keepdims=True)
    y = d * lax.rsqrt(var + EPS) * lng_ref[...] + lnb_ref[...]
    o_ref[...] = (_silu(y) * gate_ref[...].astype(F32)).astype(o_ref.dtype)


def _ln_gate(c, gate, ln_g, ln_b):
    m, d = c.shape
    row = lambda i: (i, 0)
    const = lambda i: (0, 0)
    return pl.pallas_call(
        _ln_gate_kernel,
        grid=(m // NORM_ROWS,),
        in_specs=[pl.BlockSpec((NORM_ROWS, d), row),
                  pl.BlockSpec((NORM_ROWS, d), row),
                  pl.BlockSpec((1, d), const), pl.BlockSpec((1, d), const)],
        out_specs=pl.BlockSpec((NORM_ROWS, d), row),
        out_shape=jax.ShapeDtypeStruct((m, d), BF16),
        compiler_params=_params("parallel"),
        name="ln_gate",
    )(c, gate, ln_g.reshape(1, d), ln_b.reshape(1, d))


Q_TILES = ATTN_INNER // MM_TN
KV_TILES = 2 * KV_DIM // MM_TN


def _attn_out_col_block(j):
    return jnp.where(j < Q_TILES, j,
                     jnp.where(j < Q_TILES + KV_TILES, j + Q_TILES, j - KV_TILES))


def kernel(x, norm_g, attn_w_in, attn_b_in, attn_sinks, attn_w_out, attn_b_out,
           conv_w_in, conv_b_in, conv_dw_w, conv_dw_b, conv_ln_g, conv_ln_b,
           conv_w_out, conv_b_out, final_g):
    batch, seq, d = x.shape
    m = batch * seq
    x0 = x.reshape(m, d)
    d_tiles = d // MM_TN

    h = _rmsnorm(x0, norm_g[0], BF16)
    proj = _matmul(h, attn_w_in, attn_b_in, BF16,
                   n_tiles=2 * Q_TILES + KV_TILES, out_col_block=_attn_out_col_block,
                   silu_from_tile=Q_TILES + KV_TILES, name="attn_in_proj")
    y = _attention(proj, attn_sinks[0], seq)
    x1 = _matmul(y, attn_w_out, attn_b_out, F32, n_tiles=d_tiles, resid=x0,
                 name="attn_out_proj")

    h = _rmsnorm(x1, norm_g[1], BF16)
    c = _glu_conv(h, conv_w_in, conv_b_in, conv_dw_w[0], conv_dw_b[0], seq)
    gate = _matmul(h, conv_w_in, conv_b_in, BF16, n_tiles=CONV_INNER // MM_TN,
                   w_tile0=2 * CONV_INNER // MM_TN, silu_from_tile=0,
                   name="conv_gate_proj")
    y = _ln_gate(c, gate, conv_ln_g[0], conv_ln_b[0])
    x2 = _matmul(y, conv_w_out, conv_b_out, F32, n_tiles=d_tiles, resid=x1,
                 name="conv_out_proj")

    return _rmsnorm(x2, final_g, F32).reshape(batch, seq, d)
```

```python
import functools

import jax
import jax.numpy as jnp
from jax import lax
from jax.experimental import pallas as pl
from jax.experimental.pallas import tpu as pltpu

F32 = jnp.float32
BF16 = jnp.bfloat16

D_MODEL = 4096
HEAD_DIM = 64
N_Q_HEADS = D_MODEL // HEAD_DIM
N_KV_HEADS = N_Q_HEADS // 8
GROUP = N_Q_HEADS // N_KV_HEADS
ATTN_INNER = N_Q_HEADS * HEAD_DIM
KV_DIM = N_KV_HEADS * HEAD_DIM
WINDOW = 128
BLOCK = 128
CONV_INNER = D_MODEL
CONV_WIDTH = 31
EPS = 1e-5

VMEM_LIMIT_BYTES = 56 * 1024 * 1024
LANES = 128
SUBLANES = 8

MM_TM = 1024
MM_TN = 512

NORM_ROWS = 256
LN_ROW_GROUP = 16
LN_LANE_CHUNK = 1024
LN_UNROLL = 4
CONV_HALO = 32
CONV_CHUNK = 256
CONV_ROW_GROUP = 32
GLU_PIECE_COLS = 256
GLU_K_SLICE = 512
HEADS_PER_DOT = 4
NEG_BIG = -1e30


def _params(*sem):
    return pltpu.CompilerParams(dimension_semantics=sem,
                                vmem_limit_bytes=VMEM_LIMIT_BYTES)


def _sigmoid(x):
    return 0.5 * jnp.tanh(0.5 * x) + 0.5


def _silu(x):
    return x * _sigmoid(x)


def _rmsnorm_kernel(x_ref, g_ref, o_ref):
    x = x_ref[...]
    r = lax.rsqrt(jnp.mean(x * x, axis=-1, keepdims=True) + EPS)
    o_ref[...] = (x * r * g_ref[...]).astype(o_ref.dtype)


def _rmsnorm(x, g, out_dtype):
    m, d = x.shape
    return pl.pallas_call(
        _rmsnorm_kernel,
        grid=(m // NORM_ROWS,),
        in_specs=[pl.BlockSpec((NORM_ROWS, d), lambda i: (i, 0)),
                  pl.BlockSpec((1, d), lambda i: (0, 0))],
        out_specs=pl.BlockSpec((NORM_ROWS, d), lambda i: (i, 0)),
        out_shape=jax.ShapeDtypeStruct((m, d), out_dtype),
        compiler_params=_params("parallel"),
        name="rmsnorm",
    )(x, g.reshape(1, d))


def _matmul_kernel(*refs, has_resid, silu_from_tile):
    if has_resid:
        lhs_ref, w_ref, b_ref, r_ref, o_ref, wbf_ref = refs
    else:
        lhs_ref, w_ref, b_ref, o_ref, wbf_ref = refs

    @pl.when(pl.program_id(1) == 0)
    def _():
        wbf_ref[...] = w_ref[...].astype(BF16)

    acc = jnp.dot(lhs_ref[...], wbf_ref[...], preferred_element_type=F32)
    acc = acc + b_ref[...]
    if silu_from_tile is not None:
        acc = jnp.where(pl.program_id(0) >= silu_from_tile, _silu(acc), acc)
    if has_resid:
        acc = acc + r_ref[...]
    o_ref[...] = acc.astype(o_ref.dtype)


def _matmul(lhs, w3, b2, out_dtype, *, n_tiles, w_tile0=0, resid=None,
            out_col_block=None, silu_from_tile=None, name):
    m, k = lhs.shape
    ni = m // MM_TM
    if out_col_block is None:
        out_col_block = lambda j: j
    in_specs = [
        pl.BlockSpec((MM_TM, k), lambda j, i: (i, 0)),
        pl.BlockSpec((None, k, MM_TN), lambda j, i: (0, 0, j + w_tile0)),
        pl.BlockSpec((1, MM_TN), lambda j, i: (0, j + w_tile0)),
    ]
    args = [lhs, w3, b2]
    if resid is not None:
        in_specs.append(pl.BlockSpec((MM_TM, MM_TN), lambda j, i: (i, j)))
        args.append(resid)
    return pl.pallas_call(
        functools.partial(_matmul_kernel, has_resid=resid is not None,
                          silu_from_tile=silu_from_tile),
        grid=(n_tiles, ni),
        in_specs=in_specs,
        out_specs=pl.BlockSpec((MM_TM, MM_TN),
                               lambda j, i: (i, out_col_block(j))),
        out_shape=jax.ShapeDtypeStruct((m, n_tiles * MM_TN), out_dtype),
        scratch_shapes=[pltpu.VMEM((k, MM_TN), BF16)],
        compiler_params=_params("arbitrary", "arbitrary"),
        name=name,
    )(*args)


def _attn_kernel(sink_ref, q_ref, gate_ref, k_ref, v_ref, kp_ref, vp_ref, o_ref,
                 s_ref, p_ref, vbd_ref, *, blocks_per_seq):
    n = pl.program_id(0)
    width = 2 * BLOCK
    dot_w = HEADS_PER_DOT * HEAD_DIM
    halves = GROUP // HEADS_PER_DOT
    n_dots = N_KV_HEADS * halves

    lane128 = lax.broadcasted_iota(jnp.int32, (width, LANES), 1)
    bd_lane_group = lax.broadcasted_iota(jnp.int32, (width, dot_w), 1) // HEAD_DIM

    def block_diag(pair_f32, odd, scale):
        rolled = pltpu.roll(pair_f32, HEAD_DIM, axis=1)
        if odd:
            rep = jnp.where(lane128 < HEAD_DIM, rolled, pair_f32)
        else:
            rep = jnp.where(lane128 < HEAD_DIM, pair_f32, rolled)
        rep = (rep * scale).astype(BF16)
        rep = jnp.concatenate([rep, rep], axis=1)
        zero = jnp.zeros_like(rep)
        return jnp.concatenate(
            [jnp.where(bd_lane_group == h, rep, zero) for h in range(HEADS_PER_DOT)],
            axis=0)

    def q_cols(t):
        head0 = t * HEADS_PER_DOT
        return slice(head0 * HEAD_DIM, head0 * HEAD_DIM + dot_w)

    scale = HEAD_DIM ** -0.5
    for pair in range(N_KV_HEADS // 2):
        lanes = slice(pair * LANES, (pair + 1) * LANES)
        k_pair = jnp.concatenate([kp_ref[:, lanes], k_ref[:, lanes]], axis=0).astype(F32)
        v_pair = jnp.concatenate([vp_ref[:, lanes], v_ref[:, lanes]], axis=0).astype(F32)
        for odd in range(2):
            kvh = 2 * pair + odd
            kbd_t = block_diag(k_pair, odd, scale)
            vbd_ref[kvh] = block_diag(v_pair, odd, 1.0)
            for half in range(halves):
                t = kvh * halves + half
                s_ref[t] = lax.dot_general(q_ref[:, q_cols(t)], kbd_t,
                                           (((1,), (1,)), ((), ())),
                                           preferred_element_type=F32)

    rows = n_dots * BLOCK
    r_in_block = lax.broadcasted_iota(jnp.int32, (rows, BLOCK), 0) & (BLOCK - 1)
    col = lax.broadcasted_iota(jnp.int32, (rows, BLOCK), 1)
    take_prev = col > r_in_block
    no_prev = jnp.where((n % blocks_per_seq) == 0, NEG_BIG, 0.0).astype(F32)
    fold_bias = jnp.where(take_prev, no_prev, 0.0)
    zero_p = jnp.zeros((rows, BLOCK), BF16)
    for h in range(HEADS_PER_DOT):
        lo = h * width
        s_prev = s_ref[:, :, lo:lo + BLOCK].reshape(rows, BLOCK)
        s_cur = s_ref[:, :, lo + BLOCK:lo + width].reshape(rows, BLOCK)
        sink = jnp.broadcast_to(sink_ref[h][:, None, :],
                                (n_dots, BLOCK, BLOCK)).reshape(rows, BLOCK)
        sf = jnp.where(take_prev, s_prev, s_cur) + fold_bias
        m = jnp.maximum(jnp.max(sf, axis=-1, keepdims=True), sink)
        e = jnp.exp(sf - m)
        l = jnp.sum(e, axis=-1, keepdims=True) + jnp.exp(sink - m)
        p = (e / l).astype(BF16)
        p_ref[:, :, lo:lo + BLOCK] = (
            jnp.where(take_prev, p, zero_p).reshape(n_dots, BLOCK, BLOCK))
        p_ref[:, :, lo + BLOCK:lo + width] = (
            jnp.where(take_prev, zero_p, p).reshape(n_dots, BLOCK, BLOCK))

    for t in range(n_dots):
        o = jnp.dot(p_ref[t], vbd_ref[t // halves], preferred_element_type=F32)
        cols = q_cols(t)
        o_ref[:, cols] = (o * gate_ref[:, cols].astype(F32)).astype(o_ref.dtype)


def _attention(proj, sinks, seq):
    m = proj.shape[0]
    nblk = m // BLOCK
    bps = seq // BLOCK
    kcol = 2 * ATTN_INNER // KV_DIM
    prev = lambda n: jnp.where(n % bps == 0, n, n - 1)
    n_dots = N_Q_HEADS // HEADS_PER_DOT
    keys = HEADS_PER_DOT * 2 * BLOCK
    sink_tab = jnp.broadcast_to(sinks.reshape(n_dots, HEADS_PER_DOT).T[:, :, None],
                                (HEADS_PER_DOT, n_dots, BLOCK))
    return pl.pallas_call(
        functools.partial(_attn_kernel, blocks_per_seq=bps),
        grid=(nblk,),
        in_specs=[
            pl.BlockSpec((HEADS_PER_DOT, n_dots, BLOCK), lambda n: (0, 0, 0)),
            pl.BlockSpec((BLOCK, ATTN_INNER), lambda n: (n, 0)),
            pl.BlockSpec((BLOCK, ATTN_INNER), lambda n: (n, 1)),
            pl.BlockSpec((BLOCK, KV_DIM), lambda n: (n, kcol)),
            pl.BlockSpec((BLOCK, KV_DIM), lambda n: (n, kcol + 1)),
            pl.BlockSpec((BLOCK, KV_DIM), lambda n: (prev(n), kcol)),
            pl.BlockSpec((BLOCK, KV_DIM), lambda n: (prev(n), kcol + 1)),
        ],
        out_specs=pl.BlockSpec((BLOCK, ATTN_INNER), lambda n: (n, 0)),
        out_shape=jax.ShapeDtypeStruct((m, ATTN_INNER), BF16),
        scratch_shapes=[pltpu.VMEM((n_dots, BLOCK, keys), F32),
                        pltpu.VMEM((n_dots, BLOCK, keys), BF16),
                        pltpu.VMEM((N_KV_HEADS, keys, HEADS_PER_DOT * HEAD_DIM), BF16)],
        compiler_params=_params("parallel"),
        name="swa_attention",
    )(sink_tab, proj, proj, proj, proj, proj, proj)


def _shift_up_one_row(x, keep_mask):
    r = pltpu.roll(x, SUBLANES - 1, axis=1)
    nxt = jnp.concatenate([r[1:], r[-1:]], axis=0)
    return jnp.where(keep_mask, r, nxt)


def _causal_conv_group(u_ref, wb_ref, row0, lanes, keep_mask):
    nv = CONV_ROW_GROUP // SUBLANES
    off0 = CONV_HALO - (CONV_WIDTH - 1)

    def window(row, n):
        return u_ref[pl.ds(row, n * SUBLANES), lanes].reshape(n, SUBLANES, LANES)

    def phase_sum(s, n):
        acc = None
        for q in range((off0 + CONV_WIDTH - 1) // SUBLANES + 1):
            k = q * SUBLANES + s - off0
            if 0 <= k < CONV_WIDTH:
                term = wb_ref[k, :, lanes][None] * window(row0 + q * SUBLANES, n)
                acc = term if acc is None else acc + term
        return acc

    partial = phase_sum(SUBLANES - 1, nv + 1)
    for s in range(SUBLANES - 2, 0, -1):
        partial = phase_sum(s, nv + 1) + _shift_up_one_row(partial, keep_mask)
    out = phase_sum(0, nv) + _shift_up_one_row(partial, keep_mask)[:nv]
    return out.reshape(CONV_ROW_GROUP, LANES)


def _glu_conv_kernel(h_ref, wa_ref, wg_ref, ba_ref, bg_ref, dw_ref, dwb_ref,
                     c_ref, wa_bf, wg_bf, wb_ref, halo_ref, *u_refs, tiles_per_seq):
    i = pl.program_id(1)
    tm, tn = c_ref.shape

    @pl.when(i == 0)
    def _():
        wa_bf[...] = wa_ref[...].astype(BF16)
        wg_bf[...] = wg_ref[...].astype(BF16)
        for k in range(CONV_WIDTH):
            wb_ref[k] = jnp.broadcast_to(dw_ref[k:k + 1, :], (SUBLANES, tn))

    @pl.when(i % tiles_per_seq == 0)
    def _():
        halo_ref[...] = jnp.zeros((CONV_HALO, tn), F32)

    keep_mask = lax.broadcasted_iota(
        jnp.int32, (CONV_ROW_GROUP // SUBLANES + 1, SUBLANES, LANES), 1) < SUBLANES - 1

    n_chunks = len(u_refs)
    n_pieces = tn // GLU_PIECE_COLS

    kdim = h_ref.shape[1]

    def glu_steps(c, p):
        rows = slice(c * CONV_CHUNK, (c + 1) * CONV_CHUNK)
        cols = slice(p * GLU_PIECE_COLS, (p + 1) * GLU_PIECE_COLS)
        acc = {}

        def step(w_bf, name, k0):
            ks = slice(k0, k0 + GLU_K_SLICE)
            part = jnp.dot(h_ref[rows, ks], w_bf[ks, cols], preferred_element_type=F32)
            acc[name] = part if name not in acc else acc[name] + part

        def finish():
            a = acc["a"] + ba_ref[:, cols]
            g = acc["g"] + bg_ref[:, cols]
            u_refs[c][CONV_HALO:, cols] = a * _sigmoid(g)

        steps = []
        for k0 in range(0, kdim, GLU_K_SLICE):
            steps.append(functools.partial(step, wa_bf, "a", k0))
            steps.append(functools.partial(step, wg_bf, "g", k0))
        return steps, finish

    def glu_piece(c, p):
        steps, finish = glu_steps(c, p)
        for s in steps:
            s()
        finish()

    def fill_halo(c):
        src = halo_ref[...] if c == 0 else u_refs[c - 1][CONV_CHUNK:, :]
        u_refs[c][:CONV_HALO, :] = src

    def conv_groups(c):
        for lc in range(tn // LANES):
            lanes = slice(lc * LANES, (lc + 1) * LANES)
            for r0 in range(0, CONV_CHUNK, CONV_ROW_GROUP):
                def run(lanes=lanes, r0=r0):
                    out = _causal_conv_group(u_refs[c], wb_ref, r0, lanes, keep_mask)
                    o0 = c * CONV_CHUNK + r0
                    c_ref[o0:o0 + CONV_ROW_GROUP, lanes] = out + dwb_ref[:, lanes]
                yield run

    for p in range(n_pieces):
        glu_piece(0, p)
    for c in range(n_chunks):
        fill_halo(c)
        groups = list(conv_groups(c))
        per_piece = len(groups) // n_pieces
        for p in range(n_pieces):
            mine = groups[p * per_piece:(p + 1) * per_piece]
            if c + 1 < n_chunks:
                steps, finish = glu_steps(c + 1, p)
                per_group = len(steps) // len(mine)
                for gi, run in enumerate(mine):
                    for s in steps[gi * per_group:(gi + 1) * per_group]:
                        s()
                    run()
                finish()
            else:
                for run in mine:
                    run()

    halo_ref[...] = u_refs[-1][CONV_CHUNK:, :]


def _glu_conv(h, w3, b2, dw_w, dw_b, seq):
    m, k = h.shape
    c = CONV_INNER
    nj, ni = c // MM_TN, m // MM_TM
    once = pl.Buffered(1)
    return pl.pallas_call(
        functools.partial(_glu_conv_kernel, tiles_per_seq=seq // MM_TM),
        grid=(nj, ni),
        in_specs=[
            pl.BlockSpec((MM_TM, k), lambda j, i: (i, 0)),
            pl.BlockSpec((None, k, MM_TN), lambda j, i: (0, 0, j), pipeline_mode=once),
            pl.BlockSpec((None, k, MM_TN), lambda j, i: (0, 0, j + nj), pipeline_mode=once),
            pl.BlockSpec((1, MM_TN), lambda j, i: (0, j)),
            pl.BlockSpec((1, MM_TN), lambda j, i: (0, j + nj)),
            pl.BlockSpec((CONV_WIDTH, MM_TN), lambda j, i: (0, j)),
            pl.BlockSpec((1, MM_TN), lambda j, i: (0, j)),
        ],
        out_specs=pl.BlockSpec((MM_TM, MM_TN), lambda j, i: (i, j)),
        out_shape=jax.ShapeDtypeStruct((m, c), F32),
        scratch_shapes=[pltpu.VMEM((k, MM_TN), BF16),
                        pltpu.VMEM((k, MM_TN), BF16),
                        pltpu.VMEM((CONV_WIDTH, SUBLANES, MM_TN), F32),
                        pltpu.VMEM((CONV_HALO, MM_TN), F32)]
                       + [pltpu.VMEM((CONV_HALO + CONV_CHUNK, MM_TN), F32)] * (MM_TM // CONV_CHUNK),
        compiler_params=_params("arbitrary", "arbitrary"),
        name="glu_conv_in_proj",
    )(h, w3, w3, b2, b2, dw_w, dw_b.reshape(1, c))


def _ln_gate_kernel(c_ref, gate_ref, lng_ref, lnb_ref, o_ref):
    def group(r, carry):
        rows = pl.ds(pl.multiple_of(r * LN_ROW_GROUP, LN_ROW_GROUP), LN_ROW_GROUP)
        c = c_ref[rows, :]
        mu = jnp.mean(c, axis=-1, keepdims=True)
        d = c - mu
        rstd = lax.rsqrt(jnp.mean(d * d, axis=-1, keepdims=True) + EPS)
        for l0 in range(0, c.shape[1], LN_LANE_CHUNK):
            lanes = slice(l0, l0 + LN_LANE_CHUNK)
            y = d[:, lanes] * rstd * lng_ref[:, lanes] + lnb_ref[:, lanes]
            o_ref[rows, lanes] = (_silu(y) * gate_ref[rows, lanes].astype(F32)).astype(o_ref.dtype)
        return carry

    lax.fori_loop(0, o_ref.shape[0] // LN_ROW_GROUP, group, 0, unroll=LN_UNROLL)


def _ln_gate(c, gate, ln_g, ln_b):
    m, d = c.shape
    row = lambda i: (i, 0)
    const = lambda i: (0, 0)
    return pl.pallas_call(
        _ln_gate_kernel,
        grid=(m // NORM_ROWS,),
        in_specs=[pl.BlockSpec((NORM_ROWS, d), row),
                  pl.BlockSpec((NORM_ROWS, d), row),
                  pl.BlockSpec((1, d), const), pl.BlockSpec((1, d), const)],
        out_specs=pl.BlockSpec((NORM_ROWS, d), row),
        out_shape=jax.ShapeDtypeStruct((m, d), BF16),
        compiler_params=_params("parallel"),
        name="ln_gate",
    )(c, gate, ln_g.reshape(1, d), ln_b.reshape(1, d))


Q_TILES = ATTN_INNER // MM_TN
KV_TILES = 2 * KV_DIM // MM_TN


def _attn_out_col_block(j):
    return jnp.where(j < Q_TILES, j,
                     jnp.where(j < Q_TILES + KV_TILES, j + Q_TILES, j - KV_TILES))


def kernel(x, norm_g, attn_w_in, attn_b_in, attn_sinks, attn_w_out, attn_b_out,
           conv_w_in, conv_b_in, conv_dw_w, conv_dw_b, conv_ln_g, conv_ln_b,
           conv_w_out, conv_b_out, final_g):
    batch, seq, d = x.shape
    m = batch * seq
    x0 = x.reshape(m, d)
    d_tiles = d // MM_TN

    h = _rmsnorm(x0, norm_g[0], BF16)
    proj = _matmul(h, attn_w_in, attn_b_in, BF16,
                   n_tiles=2 * Q_TILES + KV_TILES, out_col_block=_attn_out_col_block,
                   silu_from_tile=Q_TILES + KV_TILES, name="attn_in_proj")
    y = _attention(proj, attn_sinks[0], seq)
    x1 = _matmul(y, attn_w_out, attn_b_out, F32, n_tiles=d_tiles, resid=x0,
                 name="attn_out_proj")

    h = _rmsnorm(x1, norm_g[1], BF16)
    c = _glu_conv(h, conv_w_in, conv_b_in, conv_dw_w[0], conv_dw_b[0], seq)
    gate = _matmul(h, conv_w_in, conv_b_in, BF16, n_tiles=CONV_INNER // MM_TN,
                   w_tile0=2 * CONV_INNER // MM_TN, silu_from_tile=0,
                   name="conv_gate_proj")
    y = _ln_gate(c, gate, conv_ln_g[0], conv_ln_b[0])
    x2 = _matmul(y, conv_w_out, conv_b_out, F32, n_tiles=d_tiles, resid=x1,
                 name="conv_out_proj")

    return _rmsnorm(x2, final_g, F32).reshape(batch, seq, d)
```

```python
import functools

import jax
import jax.numpy as jnp
from jax import lax
from jax.experimental import pallas as pl
from jax.experimental.pallas import tpu as pltpu

F32 = jnp.float32
BF16 = jnp.bfloat16

D_MODEL = 4096
HEAD_DIM = 64
N_Q_HEADS = D_MODEL // HEAD_DIM
N_KV_HEADS = N_Q_HEADS // 8
GROUP = N_Q_HEADS // N_KV_HEADS
ATTN_INNER = N_Q_HEADS * HEAD_DIM
KV_DIM = N_KV_HEADS * HEAD_DIM
WINDOW = 128
BLOCK = 128
CONV_INNER = D_MODEL
CONV_WIDTH = 31
EPS = 1e-5

VMEM_LIMIT_BYTES = 56 * 1024 * 1024
LANES = 128
SUBLANES = 8

MM_TM = 1024
MM_TN = 512

NORM_ROWS = 512
LN_ROW_GROUP = 16
LN_LANE_CHUNK = 1024
LN_UNROLL = 4
CONV_HALO = 32
CONV_CHUNK = 256
CONV_ROW_GROUP = 32
GLU_PIECE_COLS = 256
GLU_K_SLICE = 512
HEADS_PER_DOT = 4
NEG_BIG = -1e30


def _params(*sem):
    return pltpu.CompilerParams(dimension_semantics=sem,
                                vmem_limit_bytes=VMEM_LIMIT_BYTES)


def _sigmoid(x):
    return 0.5 * jnp.tanh(0.5 * x) + 0.5


def _silu(x):
    return x * _sigmoid(x)


def _rmsnorm_kernel(x_ref, g_ref, o_ref):
    x = x_ref[...]
    r = lax.rsqrt(jnp.mean(x * x, axis=-1, keepdims=True) + EPS)
    o_ref[...] = (x * r * g_ref[...]).astype(o_ref.dtype)


def _rmsnorm(x, g, out_dtype):
    m, d = x.shape
    return pl.pallas_call(
        _rmsnorm_kernel,
        grid=(m // NORM_ROWS,),
        in_specs=[pl.BlockSpec((NORM_ROWS, d), lambda i: (i, 0)),
                  pl.BlockSpec((1, d), lambda i: (0, 0))],
        out_specs=pl.BlockSpec((NORM_ROWS, d), lambda i: (i, 0)),
        out_shape=jax.ShapeDtypeStruct((m, d), out_dtype),
        compiler_params=_params("parallel"),
        name="rmsnorm",
    )(x, g.reshape(1, d))


def _matmul_kernel(*refs, has_resid, silu):
    if has_resid:
        lhs_ref, w_ref, b_ref, r_ref, o_ref, wbf_ref = refs
    else:
        lhs_ref, w_ref, b_ref, o_ref, wbf_ref = refs

    @pl.when(pl.program_id(1) == 0)
    def _():
        wbf_ref[...] = w_ref[...].astype(BF16)

    acc = jnp.dot(lhs_ref[...], wbf_ref[...], preferred_element_type=F32)
    acc = acc + b_ref[...]
    if silu:
        acc = _silu(acc)
    if has_resid:
        acc = acc + r_ref[...]
    o_ref[...] = acc.astype(o_ref.dtype)


def _matmul(lhs, w3, b2, out_dtype, *, n_tiles, tn=MM_TN, w_tile0=0, resid=None,
            silu=False, name):
    m, k = lhs.shape
    ni = m // MM_TM
    w_mode = pl.Buffered(1) if tn > MM_TN else None
    in_specs = [
        pl.BlockSpec((MM_TM, k), lambda j, i: (i, 0)),
        pl.BlockSpec((None, k, tn), lambda j, i: (0, 0, j + w_tile0), pipeline_mode=w_mode),
        pl.BlockSpec((1, tn), lambda j, i: (0, j + w_tile0)),
    ]
    args = [lhs, w3, b2]
    if resid is not None:
        in_specs.append(pl.BlockSpec((MM_TM, tn), lambda j, i: (i, j)))
        args.append(resid)
    return pl.pallas_call(
        functools.partial(_matmul_kernel, has_resid=resid is not None, silu=silu),
        grid=(n_tiles, ni),
        in_specs=in_specs,
        out_specs=pl.BlockSpec((MM_TM, tn), lambda j, i: (i, j)),
        out_shape=jax.ShapeDtypeStruct((m, n_tiles * tn), out_dtype),
        scratch_shapes=[pltpu.VMEM((k, tn), BF16)],
        compiler_params=_params("arbitrary", "arbitrary"),
        name=name,
    )(*args)


def _attn_kernel(sink_ref, q_ref, gate_ref, k_ref, v_ref, kp_ref, vp_ref, o_ref,
                 s_ref, p_ref, vbd_ref, *, blocks_per_seq):
    n = pl.program_id(0)
    width = 2 * BLOCK
    dot_w = HEADS_PER_DOT * HEAD_DIM
    halves = GROUP // HEADS_PER_DOT
    n_dots = N_KV_HEADS * halves

    lane128 = lax.broadcasted_iota(jnp.int32, (width, LANES), 1)
    bd_lane_group = lax.broadcasted_iota(jnp.int32, (width, dot_w), 1) // HEAD_DIM

    def block_diag(pair_f32, odd, scale):
        rolled = pltpu.roll(pair_f32, HEAD_DIM, axis=1)
        if odd:
            rep = jnp.where(lane128 < HEAD_DIM, rolled, pair_f32)
        else:
            rep = jnp.where(lane128 < HEAD_DIM, pair_f32, rolled)
        rep = (rep * scale).astype(BF16)
        rep = jnp.concatenate([rep, rep], axis=1)
        zero = jnp.zeros_like(rep)
        return jnp.concatenate(
            [jnp.where(bd_lane_group == h, rep, zero) for h in range(HEADS_PER_DOT)],
            axis=0)

    def q_cols(t):
        head0 = t * HEADS_PER_DOT
        return slice(head0 * HEAD_DIM, head0 * HEAD_DIM + dot_w)

    scale = HEAD_DIM ** -0.5
    for pair in range(N_KV_HEADS // 2):
        lanes = slice(pair * LANES, (pair + 1) * LANES)
        k_pair = jnp.concatenate([kp_ref[:, lanes], k_ref[:, lanes]], axis=0).astype(F32)
        v_pair = jnp.concatenate([vp_ref[:, lanes], v_ref[:, lanes]], axis=0).astype(F32)
        for odd in range(2):
            kvh = 2 * pair + odd
            kbd_t = block_diag(k_pair, odd, scale)
            vbd_ref[kvh] = block_diag(v_pair, odd, 1.0)
            for half in range(halves):
                t = kvh * halves + half
                s_ref[t] = lax.dot_general(q_ref[:, q_cols(t)], kbd_t,
                                           (((1,), (1,)), ((), ())),
                                           preferred_element_type=F32)

    rows = n_dots * BLOCK
    r_in_block = lax.broadcasted_iota(jnp.int32, (rows, BLOCK), 0) & (BLOCK - 1)
    col = lax.broadcasted_iota(jnp.int32, (rows, BLOCK), 1)
    take_prev = col > r_in_block
    no_prev = jnp.where((n % blocks_per_seq) == 0, NEG_BIG, 0.0).astype(F32)
    fold_bias = jnp.where(take_prev, no_prev, 0.0)
    zero_p = jnp.zeros((rows, BLOCK), BF16)
    for h in range(HEADS_PER_DOT):
        lo = h * width
        s_prev = s_ref[:, :, lo:lo + BLOCK].reshape(rows, BLOCK)
        s_cur = s_ref[:, :, lo + BLOCK:lo + width].reshape(rows, BLOCK)
        sink = jnp.broadcast_to(sink_ref[h][:, None, :],
                                (n_dots, BLOCK, BLOCK)).reshape(rows, BLOCK)
        sf = jnp.where(take_prev, s_prev, s_cur) + fold_bias
        m = jnp.maximum(jnp.max(sf, axis=-1, keepdims=True), sink)
        e = jnp.exp(sf - m)
        l = jnp.sum(e, axis=-1, keepdims=True) + jnp.exp(sink - m)
        p = (e / l).astype(BF16)
        p_ref[:, :, lo:lo + BLOCK] = (
            jnp.where(take_prev, p, zero_p).reshape(n_dots, BLOCK, BLOCK))
        p_ref[:, :, lo + BLOCK:lo + width] = (
            jnp.where(take_prev, zero_p, p).reshape(n_dots, BLOCK, BLOCK))

    for t in range(n_dots):
        o = jnp.dot(p_ref[t], vbd_ref[t // halves], preferred_element_type=F32)
        cols = q_cols(t)
        o_ref[:, cols] = (o * gate_ref[:, cols].astype(F32)).astype(o_ref.dtype)


def _attention(qkv, gate, sinks, seq):
    m = qkv.shape[0]
    nblk = m // BLOCK
    bps = seq // BLOCK
    kcol = ATTN_INNER // KV_DIM
    prev = lambda n: jnp.where(n % bps == 0, n, n - 1)
    n_dots = N_Q_HEADS // HEADS_PER_DOT
    keys = HEADS_PER_DOT * 2 * BLOCK
    sink_tab = jnp.broadcast_to(sinks.reshape(n_dots, HEADS_PER_DOT).T[:, :, None],
                                (HEADS_PER_DOT, n_dots, BLOCK))
    return pl.pallas_call(
        functools.partial(_attn_kernel, blocks_per_seq=bps),
        grid=(nblk,),
        in_specs=[
            pl.BlockSpec((HEADS_PER_DOT, n_dots, BLOCK), lambda n: (0, 0, 0)),
            pl.BlockSpec((BLOCK, ATTN_INNER), lambda n: (n, 0)),
            pl.BlockSpec((BLOCK, ATTN_INNER), lambda n: (n, 0)),
            pl.BlockSpec((BLOCK, KV_DIM), lambda n: (n, kcol)),
            pl.BlockSpec((BLOCK, KV_DIM), lambda n: (n, kcol + 1)),
            pl.BlockSpec((BLOCK, KV_DIM), lambda n: (prev(n), kcol)),
            pl.BlockSpec((BLOCK, KV_DIM), lambda n: (prev(n), kcol + 1)),
        ],
        out_specs=pl.BlockSpec((BLOCK, ATTN_INNER), lambda n: (n, 0)),
        out_shape=jax.ShapeDtypeStruct((m, ATTN_INNER), BF16),
        scratch_shapes=[pltpu.VMEM((n_dots, BLOCK, keys), F32),
                        pltpu.VMEM((n_dots, BLOCK, keys), BF16),
                        pltpu.VMEM((N_KV_HEADS, keys, HEADS_PER_DOT * HEAD_DIM), BF16)],
        compiler_params=_params("parallel"),
        name="swa_attention",
    )(sink_tab, qkv, gate, qkv, qkv, qkv, qkv)


def _shift_up_one_row(x, keep_mask):
    r = pltpu.roll(x, SUBLANES - 1, axis=1)
    nxt = jnp.concatenate([r[1:], r[-1:]], axis=0)
    return jnp.where(keep_mask, r, nxt)


def _causal_conv_group(u_ref, wb_ref, row0, lanes, keep_mask):
    nv = CONV_ROW_GROUP // SUBLANES
    off0 = CONV_HALO - (CONV_WIDTH - 1)

    def window(row, n):
        return u_ref[pl.ds(row, n * SUBLANES), lanes].reshape(n, SUBLANES, LANES)

    def phase_sum(s, n):
        acc = None
        for q in range((off0 + CONV_WIDTH - 1) // SUBLANES + 1):
            k = q * SUBLANES + s - off0
            if 0 <= k < CONV_WIDTH:
                term = wb_ref[k, :, lanes][None] * window(row0 + q * SUBLANES, n)
                acc = term if acc is None else acc + term
        return acc

    partial = phase_sum(SUBLANES - 1, nv + 1)
    for s in range(SUBLANES - 2, 0, -1):
        partial = phase_sum(s, nv + 1) + _shift_up_one_row(partial, keep_mask)
    out = phase_sum(0, nv) + _shift_up_one_row(partial, keep_mask)[:nv]
    return out.reshape(CONV_ROW_GROUP, LANES)


def _glu_conv_kernel(h_ref, wa_ref, wg_ref, ba_ref, bg_ref, dw_ref, dwb_ref,
                     c_ref, wa_bf, wg_bf, wb_ref, halo_ref, *u_refs, tiles_per_seq):
    i = pl.program_id(1)
    tm, tn = c_ref.shape

    @pl.when(i == 0)
    def _():
        wa_bf[...] = wa_ref[...].astype(BF16)
        wg_bf[...] = wg_ref[...].astype(BF16)
        for k in range(CONV_WIDTH):
            wb_ref[k] = jnp.broadcast_to(dw_ref[k:k + 1, :], (SUBLANES, tn))

    @pl.when(i % tiles_per_seq == 0)
    def _():
        halo_ref[...] = jnp.zeros((CONV_HALO, tn), F32)

    keep_mask = lax.broadcasted_iota(
        jnp.int32, (CONV_ROW_GROUP // SUBLANES + 1, SUBLANES, LANES), 1) < SUBLANES - 1

    n_chunks = len(u_refs)
    n_pieces = tn // GLU_PIECE_COLS

    kdim = h_ref.shape[1]

    def glu_steps(c, p):
        rows = slice(c * CONV_CHUNK, (c + 1) * CONV_CHUNK)
        cols = slice(p * GLU_PIECE_COLS, (p + 1) * GLU_PIECE_COLS)
        acc = {}

        def step(w_bf, name, k0):
            ks = slice(k0, k0 + GLU_K_SLICE)
            part = jnp.dot(h_ref[rows, ks], w_bf[ks, cols], preferred_element_type=F32)
            acc[name] = part if name not in acc else acc[name] + part

        def finish():
            a = acc["a"] + ba_ref[:, cols]
            g = acc["g"] + bg_ref[:, cols]
            u_refs[c][CONV_HALO:, cols] = a * _sigmoid(g)

        steps = []
        for k0 in range(0, kdim, GLU_K_SLICE):
            steps.append(functools.partial(step, wa_bf, "a", k0))
            steps.append(functools.partial(step, wg_bf, "g", k0))
        return steps, finish

    def glu_piece(c, p):
        steps, finish = glu_steps(c, p)
        for s in steps:
            s()
        finish()

    def fill_halo(c):
        src = halo_ref[...] if c == 0 else u_refs[c - 1][CONV_CHUNK:, :]
        u_refs[c][:CONV_HALO, :] = src

    def conv_groups(c):
        for lc in range(tn // LANES):
            lanes = slice(lc * LANES, (lc + 1) * LANES)
            for r0 in range(0, CONV_CHUNK, CONV_ROW_GROUP):
                def run(lanes=lanes, r0=r0):
                    out = _causal_conv_group(u_refs[c], wb_ref, r0, lanes, keep_mask)
                    o0 = c * CONV_CHUNK + r0
                    c_ref[o0:o0 + CONV_ROW_GROUP, lanes] = out + dwb_ref[:, lanes]
                yield run

    for p in range(n_pieces):
        glu_piece(0, p)
    for c in range(n_chunks):
        fill_halo(c)
        groups = list(conv_groups(c))
        per_piece = len(groups) // n_pieces
        for p in range(n_pieces):
            mine = groups[p * per_piece:(p + 1) * per_piece]
            if c + 1 < n_chunks:
                steps, finish = glu_steps(c + 1, p)
                per_group = len(steps) // len(mine)
                for gi, run in enumerate(mine):
                    for s in steps[gi * per_group:(gi + 1) * per_group]:
                        s()
                    run()
                finish()
            else:
                for run in mine:
                    run()

    halo_ref[...] = u_refs[-1][CONV_CHUNK:, :]


def _glu_conv(h, w3, b2, dw_w, dw_b, seq):
    m, k = h.shape
    c = CONV_INNER
    nj, ni = c // MM_TN, m // MM_TM
    once = pl.Buffered(1)
    return pl.pallas_call(
        functools.partial(_glu_conv_kernel, tiles_per_seq=seq // MM_TM),
        grid=(nj, ni),
        in_specs=[
            pl.BlockSpec((MM_TM, k), lambda j, i: (i, 0)),
            pl.BlockSpec((None, k, MM_TN), lambda j, i: (0, 0, j), pipeline_mode=once),
            pl.BlockSpec((None, k, MM_TN), lambda j, i: (0, 0, j + nj), pipeline_mode=once),
            pl.BlockSpec((1, MM_TN), lambda j, i: (0, j)),
            pl.BlockSpec((1, MM_TN), lambda j, i: (0, j + nj)),
            pl.BlockSpec((CONV_WIDTH, MM_TN), lambda j, i: (0, j)),
            pl.BlockSpec((1, MM_TN), lambda j, i: (0, j)),
        ],
        out_specs=pl.BlockSpec((MM_TM, MM_TN), lambda j, i: (i, j)),
        out_shape=jax.ShapeDtypeStruct((m, c), F32),
        scratch_shapes=[pltpu.VMEM((k, MM_TN), BF16),
                        pltpu.VMEM((k, MM_TN), BF16),
                        pltpu.VMEM((CONV_WIDTH, SUBLANES, MM_TN), F32),
                        pltpu.VMEM((CONV_HALO, MM_TN), F32)]
                       + [pltpu.VMEM((CONV_HALO + CONV_CHUNK, MM_TN), F32)] * (MM_TM // CONV_CHUNK),
        compiler_params=_params("arbitrary", "arbitrary"),
        name="glu_conv_in_proj",
    )(h, w3, w3, b2, b2, dw_w, dw_b.reshape(1, c))


def _ln_gate_kernel(c_ref, gate_ref, lng_ref, lnb_ref, o_ref):
    def group(r, carry):
        rows = pl.ds(pl.multiple_of(r * LN_ROW_GROUP, LN_ROW_GROUP), LN_ROW_GROUP)
        c = c_ref[rows, :]
        mu = jnp.mean(c, axis=-1, keepdims=True)
        d = c - mu
        rstd = lax.rsqrt(jnp.mean(d * d, axis=-1, keepdims=True) + EPS)
        for l0 in range(0, c.shape[1], LN_LANE_CHUNK):
            lanes = slice(l0, l0 + LN_LANE_CHUNK)
            y = d[:, lanes] * rstd * lng_ref[:, lanes] + lnb_ref[:, lanes]
            o_ref[rows, lanes] = (_silu(y) * gate_ref[rows, lanes].astype(F32)).astype(o_ref.dtype)
        return carry

    lax.fori_loop(0, o_ref.shape[0] // LN_ROW_GROUP, group, 0, unroll=LN_UNROLL)


def _ln_gate(c, gate, ln_g, ln_b):
    m, d = c.shape
    row = lambda i: (i, 0)
    const = lambda i: (0, 0)
    return pl.pallas_call(
        _ln_gate_kernel,
        grid=(m // NORM_ROWS,),
        in_specs=[pl.BlockSpec((NORM_ROWS, d), row),
                  pl.BlockSpec((NORM_ROWS, d), row),
                  pl.BlockSpec((1, d), const), pl.BlockSpec((1, d), const)],
        out_specs=pl.BlockSpec((NORM_ROWS, d), row),
        out_shape=jax.ShapeDtypeStruct((m, d), BF16),
        compiler_params=_params("parallel"),
        name="ln_gate",
    )(c, gate, ln_g.reshape(1, d), ln_b.reshape(1, d))


IN_TN = 1024


def kernel(x, norm_g, attn_w_in, attn_b_in, attn_sinks, attn_w_out, attn_b_out,
           conv_w_in, conv_b_in, conv_dw_w, conv_dw_b, conv_ln_g, conv_ln_b,
           conv_w_out, conv_b_out, final_g):
    batch, seq, d = x.shape
    m = batch * seq
    x0 = x.reshape(m, d)
    d_tiles = d // MM_TN

    h = _rmsnorm(x0, norm_g[0], BF16)
    qkv_tiles = (ATTN_INNER + 2 * KV_DIM) // IN_TN
    qkv = _matmul(h, attn_w_in, attn_b_in, BF16, n_tiles=qkv_tiles, tn=IN_TN,
                  name="attn_qkv_proj")
    gate = _matmul(h, attn_w_in, attn_b_in, BF16, n_tiles=ATTN_INNER // IN_TN, tn=IN_TN,
                   w_tile0=qkv_tiles, silu=True, name="attn_gate_proj")
    y = _attention(qkv, gate, attn_sinks[0], seq)
    x1 = _matmul(y, attn_w_out, attn_b_out, F32, n_tiles=d_tiles, resid=x0,
                 name="attn_out_proj")

    h = _rmsnorm(x1, norm_g[1], BF16)
    c = _glu_conv(h, conv_w_in, conv_b_in, conv_dw_w[0], conv_dw_b[0], seq)
    gate = _matmul(h, conv_w_in, conv_b_in, BF16, n_tiles=CONV_INNER // IN_TN, tn=IN_TN,
                   w_tile0=2 * CONV_INNER // IN_TN, silu=True, name="conv_gate_proj")
    y = _ln_gate(c, gate, conv_ln_g[0], conv_ln_b[0])
    x2 = _matmul(y, conv_w_out, conv_b_out, F32, n_tiles=d_tiles, resid=x1,
                 name="conv_out_proj")

    return _rmsnorm(x2, final_g, F32).reshape(batch, seq, d)
```

```python
import functools

import jax
import jax.numpy as jnp
from jax import lax
from jax.experimental import pallas as pl
from jax.experimental.pallas import tpu as pltpu

F32 = jnp.float32
BF16 = jnp.bfloat16

D_MODEL = 4096
HEAD_DIM = 64
N_Q_HEADS = D_MODEL // HEAD_DIM
N_KV_HEADS = N_Q_HEADS // 8
GROUP = N_Q_HEADS // N_KV_HEADS
ATTN_INNER = N_Q_HEADS * HEAD_DIM
KV_DIM = N_KV_HEADS * HEAD_DIM
WINDOW = 128
BLOCK = 128
CONV_INNER = D_MODEL
CONV_WIDTH = 31
EPS = 1e-5

VMEM_LIMIT_BYTES = 56 * 1024 * 1024
LANES = 128
SUBLANES = 8

MM_TM = 1024
MM_TN = 512

NORM_ROWS = 512
LN_ROW_GROUP = 16
LN_LANE_CHUNK = 1024
LN_UNROLL = 4
GLU_TN = 256
GLU_K_SLICE = 512
CONV_HALO = 32
CONV_CHUNK = 256
CONV_ROW_GROUP = 32
HEADS_PER_DOT = 4
NEG_BIG = -1e30


def _params(*sem):
    return pltpu.CompilerParams(dimension_semantics=sem,
                                vmem_limit_bytes=VMEM_LIMIT_BYTES)


def _sigmoid(x):
    return 0.5 * jnp.tanh(0.5 * x) + 0.5


def _silu(x):
    return x * _sigmoid(x)


def _rmsnorm_kernel(x_ref, g_ref, o_ref):
    x = x_ref[...]
    r = lax.rsqrt(jnp.mean(x * x, axis=-1, keepdims=True) + EPS)
    o_ref[...] = (x * r * g_ref[...]).astype(o_ref.dtype)


def _rmsnorm(x, g, out_dtype):
    m, d = x.shape
    return pl.pallas_call(
        _rmsnorm_kernel,
        grid=(m // NORM_ROWS,),
        in_specs=[pl.BlockSpec((NORM_ROWS, d), lambda i: (i, 0)),
                  pl.BlockSpec((1, d), lambda i: (0, 0))],
        out_specs=pl.BlockSpec((NORM_ROWS, d), lambda i: (i, 0)),
        out_shape=jax.ShapeDtypeStruct((m, d), out_dtype),
        compiler_params=_params("parallel"),
        name="rmsnorm",
    )(x, g.reshape(1, d))


def _matmul_kernel(*refs, has_resid, silu_from_tile):
    if has_resid:
        lhs_ref, w_ref, b_ref, r_ref, o_ref, wbf_ref = refs
    else:
        lhs_ref, w_ref, b_ref, o_ref, wbf_ref = refs

    @pl.when(pl.program_id(1) == 0)
    def _():
        wbf_ref[...] = w_ref[...].astype(BF16)

    acc = jnp.dot(lhs_ref[...], wbf_ref[...], preferred_element_type=F32)
    acc = acc + b_ref[...]
    if silu_from_tile is not None:
        acc = jnp.where(pl.program_id(0) >= silu_from_tile, _silu(acc), acc)
    if has_resid:
        acc = acc + r_ref[...]
    o_ref[...] = acc.astype(o_ref.dtype)


def _matmul(lhs, w3, b2, out_dtype, *, n_tiles, resid=None,
            out_col_block=None, silu_from_tile=None, name):
    m, k = lhs.shape
    ni = m // MM_TM
    if out_col_block is None:
        out_col_block = lambda j: j
    in_specs = [
        pl.BlockSpec((MM_TM, k), lambda j, i: (i, 0)),
        pl.BlockSpec((None, k, MM_TN), lambda j, i: (0, 0, j)),
        pl.BlockSpec((1, MM_TN), lambda j, i: (0, j)),
    ]
    args = [lhs, w3, b2]
    if resid is not None:
        in_specs.append(pl.BlockSpec((MM_TM, MM_TN), lambda j, i: (i, j)))
        args.append(resid)
    return pl.pallas_call(
        functools.partial(_matmul_kernel, has_resid=resid is not None,
                          silu_from_tile=silu_from_tile),
        grid=(n_tiles, ni),
        in_specs=in_specs,
        out_specs=pl.BlockSpec((MM_TM, MM_TN),
                               lambda j, i: (i, out_col_block(j))),
        out_shape=jax.ShapeDtypeStruct((m, n_tiles * MM_TN), out_dtype),
        scratch_shapes=[pltpu.VMEM((k, MM_TN), BF16)],
        compiler_params=_params("arbitrary", "arbitrary"),
        name=name,
    )(*args)


def _attn_kernel(sink_ref, q_ref, gate_ref, k_ref, v_ref, kp_ref, vp_ref, o_ref,
                 s_ref, p_ref, vbd_ref, *, blocks_per_seq):
    n = pl.program_id(0)
    width = 2 * BLOCK
    dot_w = HEADS_PER_DOT * HEAD_DIM
    halves = GROUP // HEADS_PER_DOT
    n_dots = N_KV_HEADS * halves

    lane128 = lax.broadcasted_iota(jnp.int32, (width, LANES), 1)
    bd_lane_group = lax.broadcasted_iota(jnp.int32, (width, dot_w), 1) // HEAD_DIM

    def block_diag(pair_f32, odd, scale):
        rolled = pltpu.roll(pair_f32, HEAD_DIM, axis=1)
        if odd:
            rep = jnp.where(lane128 < HEAD_DIM, rolled, pair_f32)
        else:
            rep = jnp.where(lane128 < HEAD_DIM, pair_f32, rolled)
        rep = (rep * scale).astype(BF16)
        rep = jnp.concatenate([rep, rep], axis=1)
        zero = jnp.zeros_like(rep)
        return jnp.concatenate(
            [jnp.where(bd_lane_group == h, rep, zero) for h in range(HEADS_PER_DOT)],
            axis=0)

    def q_cols(t):
        head0 = t * HEADS_PER_DOT
        return slice(head0 * HEAD_DIM, head0 * HEAD_DIM + dot_w)

    scale = HEAD_DIM ** -0.5
    for pair in range(N_KV_HEADS // 2):
        lanes = slice(pair * LANES, (pair + 1) * LANES)
        k_pair = jnp.concatenate([kp_ref[:, lanes], k_ref[:, lanes]], axis=0).astype(F32)
        v_pair = jnp.concatenate([vp_ref[:, lanes], v_ref[:, lanes]], axis=0).astype(F32)
        for odd in range(2):
            kvh = 2 * pair + odd
            kbd_t = block_diag(k_pair, odd, scale)
            vbd_ref[kvh] = block_diag(v_pair, odd, 1.0)
            for half in range(halves):
                t = kvh * halves + half
                s_ref[t] = lax.dot_general(q_ref[:, q_cols(t)], kbd_t,
                                           (((1,), (1,)), ((), ())),
                                           preferred_element_type=F32)

    rows = n_dots * BLOCK
    r_in_block = lax.broadcasted_iota(jnp.int32, (rows, BLOCK), 0) & (BLOCK - 1)
    col = lax.broadcasted_iota(jnp.int32, (rows, BLOCK), 1)
    take_prev = col > r_in_block
    no_prev = jnp.where((n % blocks_per_seq) == 0, NEG_BIG, 0.0).astype(F32)
    fold_bias = jnp.where(take_prev, no_prev, 0.0)
    zero_p = jnp.zeros((rows, BLOCK), BF16)
    for h in range(HEADS_PER_DOT):
        lo = h * width
        s_prev = s_ref[:, :, lo:lo + BLOCK].reshape(rows, BLOCK)
        s_cur = s_ref[:, :, lo + BLOCK:lo + width].reshape(rows, BLOCK)
        sink = jnp.broadcast_to(sink_ref[h][:, None, :],
                                (n_dots, BLOCK, BLOCK)).reshape(rows, BLOCK)
        sf = jnp.where(take_prev, s_prev, s_cur) + fold_bias
        m = jnp.maximum(jnp.max(sf, axis=-1, keepdims=True), sink)
        e = jnp.exp(sf - m)
        l = jnp.sum(e, axis=-1, keepdims=True) + jnp.exp(sink - m)
        p = (e / l).astype(BF16)
        p_ref[:, :, lo:lo + BLOCK] = (
            jnp.where(take_prev, p, zero_p).reshape(n_dots, BLOCK, BLOCK))
        p_ref[:, :, lo + BLOCK:lo + width] = (
            jnp.where(take_prev, zero_p, p).reshape(n_dots, BLOCK, BLOCK))

    for t in range(n_dots):
        o = jnp.dot(p_ref[t], vbd_ref[t // halves], preferred_element_type=F32)
        cols = q_cols(t)
        o_ref[:, cols] = (o * gate_ref[:, cols].astype(F32)).astype(o_ref.dtype)


def _attention(proj, sinks, seq):
    m = proj.shape[0]
    nblk = m // BLOCK
    bps = seq // BLOCK
    kcol = 2 * ATTN_INNER // KV_DIM
    prev = lambda n: jnp.where(n % bps == 0, n, n - 1)
    n_dots = N_Q_HEADS // HEADS_PER_DOT
    keys = HEADS_PER_DOT * 2 * BLOCK
    sink_tab = jnp.broadcast_to(sinks.reshape(n_dots, HEADS_PER_DOT).T[:, :, None],
                                (HEADS_PER_DOT, n_dots, BLOCK))
    return pl.pallas_call(
        functools.partial(_attn_kernel, blocks_per_seq=bps),
        grid=(nblk,),
        in_specs=[
            pl.BlockSpec((HEADS_PER_DOT, n_dots, BLOCK), lambda n: (0, 0, 0)),
            pl.BlockSpec((BLOCK, ATTN_INNER), lambda n: (n, 0)),
            pl.BlockSpec((BLOCK, ATTN_INNER), lambda n: (n, 1)),
            pl.BlockSpec((BLOCK, KV_DIM), lambda n: (n, kcol)),
            pl.BlockSpec((BLOCK, KV_DIM), lambda n: (n, kcol + 1)),
            pl.BlockSpec((BLOCK, KV_DIM), lambda n: (prev(n), kcol)),
            pl.BlockSpec((BLOCK, KV_DIM), lambda n: (prev(n), kcol + 1)),
        ],
        out_specs=pl.BlockSpec((BLOCK, ATTN_INNER), lambda n: (n, 0)),
        out_shape=jax.ShapeDtypeStruct((m, ATTN_INNER), BF16),
        scratch_shapes=[pltpu.VMEM((n_dots, BLOCK, keys), F32),
                        pltpu.VMEM((n_dots, BLOCK, keys), BF16),
                        pltpu.VMEM((N_KV_HEADS, keys, HEADS_PER_DOT * HEAD_DIM), BF16)],
        compiler_params=_params("parallel"),
        name="swa_attention",
    )(sink_tab, proj, proj, proj, proj, proj, proj)


def _shift_up_one_row(x, keep_mask):
    r = pltpu.roll(x, SUBLANES - 1, axis=1)
    nxt = jnp.concatenate([r[1:], r[-1:]], axis=0)
    return jnp.where(keep_mask, r, nxt)


def _causal_conv_group(u_ref, wb_ref, row0, lanes, keep_mask):
    nv = CONV_ROW_GROUP // SUBLANES
    off0 = CONV_HALO - (CONV_WIDTH - 1)

    def window(row, n):
        return u_ref[pl.ds(row, n * SUBLANES), lanes].reshape(n, SUBLANES, LANES)

    def phase_sum(s, n):
        acc = None
        for q in range((off0 + CONV_WIDTH - 1) // SUBLANES + 1):
            k = q * SUBLANES + s - off0
            if 0 <= k < CONV_WIDTH:
                term = wb_ref[k, :, lanes][None] * window(row0 + q * SUBLANES, n)
                acc = term if acc is None else acc + term
        return acc

    partial = phase_sum(SUBLANES - 1, nv + 1)
    for s in range(SUBLANES - 2, 0, -1):
        partial = phase_sum(s, nv + 1) + _shift_up_one_row(partial, keep_mask)
    out = phase_sum(0, nv) + _shift_up_one_row(partial, keep_mask)[:nv]
    return out.reshape(CONV_ROW_GROUP, LANES)


def _glu_conv_kernel(h_ref, wa_ref, wg_ref, wz_ref, ba_ref, bg_ref, bz_ref, dw_ref, dwb_ref,
                     c_ref, gate_ref, wa_bf, wg_bf, wz_bf, wb_ref, halo_ref, *u_refs,
                     tiles_per_seq):
    i = pl.program_id(1)
    tm, tn = c_ref.shape

    @pl.when(i == 0)
    def _():
        wa_bf[...] = wa_ref[...].astype(BF16)
        wg_bf[...] = wg_ref[...].astype(BF16)
        wz_bf[...] = wz_ref[...].astype(BF16)
        for k in range(CONV_WIDTH):
            wb_ref[k] = jnp.broadcast_to(dw_ref[k:k + 1, :], (SUBLANES, tn))

    @pl.when(i % tiles_per_seq == 0)
    def _():
        halo_ref[...] = jnp.zeros((CONV_HALO, tn), F32)

    keep_mask = lax.broadcasted_iota(
        jnp.int32, (CONV_ROW_GROUP // SUBLANES + 1, SUBLANES, LANES), 1) < SUBLANES - 1

    n_chunks = len(u_refs)
    kdim = h_ref.shape[1]

    def dot_steps(c, w_bfs, finish):
        rows = slice(c * CONV_CHUNK, (c + 1) * CONV_CHUNK)
        acc = [None] * len(w_bfs)

        def step(n, k0):
            ks = slice(k0, k0 + GLU_K_SLICE)
            part = jnp.dot(h_ref[rows, ks], w_bfs[n][ks, :], preferred_element_type=F32)
            acc[n] = part if acc[n] is None else acc[n] + part

        steps = []
        for k0 in range(0, kdim, GLU_K_SLICE):
            for n in range(len(w_bfs)):
                steps.append(functools.partial(step, n, k0))
        steps.append(lambda: finish(rows, *acc))
        return steps

    def glu_steps(c):
        def finish(rows, a, g):
            u_refs[c][CONV_HALO:, :] = (a + ba_ref[...]) * _sigmoid(g + bg_ref[...])
        return dot_steps(c, (wa_bf, wg_bf), finish)

    def gate_steps(c):
        def finish(rows, z):
            gate_ref[rows, :] = _silu(z + bz_ref[...]).astype(gate_ref.dtype)
        return dot_steps(c, (wz_bf,), finish)

    def fill_halo(c):
        src = halo_ref[...] if c == 0 else u_refs[c - 1][CONV_CHUNK:, :]
        u_refs[c][:CONV_HALO, :] = src

    def conv_steps(c):
        steps = []
        for lc in range(tn // LANES):
            lanes = slice(lc * LANES, (lc + 1) * LANES)
            for r0 in range(0, CONV_CHUNK, CONV_ROW_GROUP):
                def run(lanes=lanes, r0=r0):
                    out = _causal_conv_group(u_refs[c], wb_ref, r0, lanes, keep_mask)
                    o0 = c * CONV_CHUNK + r0
                    c_ref[o0:o0 + CONV_ROW_GROUP, lanes] = out + dwb_ref[:, lanes]
                steps.append(run)
        return steps

    def interleave(mxu_steps, vpu_steps):
        n = len(vpu_steps)
        for t in range(n):
            for f in mxu_steps[t * len(mxu_steps) // n:(t + 1) * len(mxu_steps) // n]:
                f()
            vpu_steps[t]()

    for f in glu_steps(0):
        f()
    for c in range(n_chunks):
        fill_halo(c)
        if c + 1 < n_chunks:
            interleave(glu_steps(c + 1), conv_steps(c))
        else:
            gate = [f for r in range(n_chunks) for f in gate_steps(r)]
            interleave(gate, conv_steps(c))

    halo_ref[...] = u_refs[-1][CONV_CHUNK:, :]


def _glu_conv(h, w3, b2, dw_w, dw_b, seq):
    m, k = h.shape
    c = CONV_INNER
    tn = GLU_TN
    nj, ni = c // tn, m // MM_TM
    w_spec = lambda part: pl.BlockSpec((None, k, tn), lambda j, i: (0, 0, j + part * nj))
    b_spec = lambda part: pl.BlockSpec((1, tn), lambda j, i: (0, j + part * nj))
    out_spec = pl.BlockSpec((MM_TM, tn), lambda j, i: (i, j))
    return pl.pallas_call(
        functools.partial(_glu_conv_kernel, tiles_per_seq=seq // MM_TM),
        grid=(nj, ni),
        in_specs=[
            pl.BlockSpec((MM_TM, k), lambda j, i: (i, 0)),
            w_spec(0), w_spec(1), w_spec(2),
            b_spec(0), b_spec(1), b_spec(2),
            pl.BlockSpec((CONV_WIDTH, tn), lambda j, i: (0, j)),
            pl.BlockSpec((1, tn), lambda j, i: (0, j)),
        ],
        out_specs=[out_spec, out_spec],
        out_shape=[jax.ShapeDtypeStruct((m, c), F32), jax.ShapeDtypeStruct((m, c), BF16)],
        scratch_shapes=[pltpu.VMEM((k, tn), BF16)] * 3
                       + [pltpu.VMEM((CONV_WIDTH, SUBLANES, tn), F32),
                          pltpu.VMEM((CONV_HALO, tn), F32)]
                       + [pltpu.VMEM((CONV_HALO + CONV_CHUNK, tn), F32)] * (MM_TM // CONV_CHUNK),
        compiler_params=_params("arbitrary", "arbitrary"),
        name="glu_conv_gate_in_proj",
    )(h, w3, w3, w3, b2, b2, b2, dw_w, dw_b.reshape(1, c))


def _ln_gate_kernel(c_ref, gate_ref, lng_ref, lnb_ref, o_ref):
    def group(r, carry):
        rows = pl.ds(pl.multiple_of(r * LN_ROW_GROUP, LN_ROW_GROUP), LN_ROW_GROUP)
        c = c_ref[rows, :]
        mu = jnp.mean(c, axis=-1, keepdims=True)
        d = c - mu
        rstd = lax.rsqrt(jnp.mean(d * d, axis=-1, keepdims=True) + EPS)
        for l0 in range(0, c.shape[1], LN_LANE_CHUNK):
            lanes = slice(l0, l0 + LN_LANE_CHUNK)
            y = d[:, lanes] * rstd * lng_ref[:, lanes] + lnb_ref[:, lanes]
            o_ref[rows, lanes] = (_silu(y) * gate_ref[rows, lanes].astype(F32)).astype(o_ref.dtype)
        return carry

    lax.fori_loop(0, o_ref.shape[0] // LN_ROW_GROUP, group, 0, unroll=LN_UNROLL)


def _ln_gate(c, gate, ln_g, ln_b):
    m, d = c.shape
    row = lambda i: (i, 0)
    const = lambda i: (0, 0)
    return pl.pallas_call(
        _ln_gate_kernel,
        grid=(m // NORM_ROWS,),
        in_specs=[pl.BlockSpec((NORM_ROWS, d), row),
                  pl.BlockSpec((NORM_ROWS, d), row),
                  pl.BlockSpec((1, d), const), pl.BlockSpec((1, d), const)],
        out_specs=pl.BlockSpec((NORM_ROWS, d), row),
        out_shape=jax.ShapeDtypeStruct((m, d), BF16),
        compiler_params=_params("parallel"),
        name="ln_gate",
    )(c, gate, ln_g.reshape(1, d), ln_b.reshape(1, d))


Q_TILES = ATTN_INNER // MM_TN
KV_TILES = 2 * KV_DIM // MM_TN


def _attn_out_col_block(j):
    return jnp.where(j < Q_TILES, j,
                     jnp.where(j < Q_TILES + KV_TILES, j + Q_TILES, j - KV_TILES))


def kernel(x, norm_g, attn_w_in, attn_b_in, attn_sinks, attn_w_out, attn_b_out,
           conv_w_in, conv_b_in, conv_dw_w, conv_dw_b, conv_ln_g, conv_ln_b,
           conv_w_out, conv_b_out, final_g):
    batch, seq, d = x.shape
    m = batch * seq
    x0 = x.reshape(m, d)
    d_tiles = d // MM_TN

    h = _rmsnorm(x0, norm_g[0], BF16)
    proj = _matmul(h, attn_w_in, attn_b_in, BF16,
                   n_tiles=2 * Q_TILES + KV_TILES, out_col_block=_attn_out_col_block,
                   silu_from_tile=Q_TILES + KV_TILES, name="attn_in_proj")
    y = _attention(proj, attn_sinks[0], seq)
    x1 = _matmul(y, attn_w_out, attn_b_out, F32, n_tiles=d_tiles, resid=x0,
                 name="attn_out_proj")

    h = _rmsnorm(x1, norm_g[1], BF16)
    c, gate = _glu_conv(h, conv_w_in, conv_b_in, conv_dw_w[0], conv_dw_b[0], seq)
    y = _ln_gate(c, gate, conv_ln_g[0], conv_ln_b[0])
    x2 = _matmul(y, conv_w_out, conv_b_out, F32, n_tiles=d_tiles, resid=x1,
                 name="conv_out_proj")

    return _rmsnorm(x2, final_g, F32).reshape(batch, seq, d)
```

```python
import functools

import jax
import jax.numpy as jnp
from jax import lax
from jax.experimental import pallas as pl
from jax.experimental.pallas import tpu as pltpu

F32 = jnp.float32
BF16 = jnp.bfloat16

D_MODEL = 4096
HEAD_DIM = 64
N_Q_HEADS = D_MODEL // HEAD_DIM
N_KV_HEADS = N_Q_HEADS // 8
GROUP = N_Q_HEADS // N_KV_HEADS
ATTN_INNER = N_Q_HEADS * HEAD_DIM
KV_DIM = N_KV_HEADS * HEAD_DIM
WINDOW = 128
BLOCK = 128
CONV_INNER = D_MODEL
CONV_WIDTH = 31
EPS = 1e-5

VMEM_LIMIT_BYTES = 56 * 1024 * 1024
LANES = 128
SUBLANES = 8

MM_TM = 1024
MM_TN = 512

NORM_ROWS = 512
LN_ROW_GROUP = 16
LN_LANE_CHUNK = 1024
LN_UNROLL = 4
GLU_TN = 256
GLU_K_SLICE = 512
CONV_HALO = 32
CONV_CHUNK = 256
CONV_ROW_GROUP = 32
HEADS_PER_DOT = 4
NEG_BIG = -1e30


def _params(*sem):
    return pltpu.CompilerParams(dimension_semantics=sem,
                                vmem_limit_bytes=VMEM_LIMIT_BYTES)


def _sigmoid(x):
    return 0.5 * jnp.tanh(0.5 * x) + 0.5


def _silu(x):
    return x * _sigmoid(x)


def _rmsnorm_kernel(x_ref, g_ref, o_ref):
    x = x_ref[...]
    r = lax.rsqrt(jnp.mean(x * x, axis=-1, keepdims=True) + EPS)
    o_ref[...] = (x * r * g_ref[...]).astype(o_ref.dtype)


def _rmsnorm(x, g, out_dtype):
    m, d = x.shape
    return pl.pallas_call(
        _rmsnorm_kernel,
        grid=(m // NORM_ROWS,),
        in_specs=[pl.BlockSpec((NORM_ROWS, d), lambda i: (i, 0)),
                  pl.BlockSpec((1, d), lambda i: (0, 0))],
        out_specs=pl.BlockSpec((NORM_ROWS, d), lambda i: (i, 0)),
        out_shape=jax.ShapeDtypeStruct((m, d), out_dtype),
        compiler_params=_params("parallel"),
        name="rmsnorm",
    )(x, g.reshape(1, d))


def _matmul_kernel(*refs, has_resid, silu):
    if has_resid:
        lhs_ref, w_ref, b_ref, r_ref, o_ref, wbf_ref = refs
    else:
        lhs_ref, w_ref, b_ref, o_ref, wbf_ref = refs

    @pl.when(pl.program_id(1) == 0)
    def _():
        wbf_ref[...] = w_ref[...].astype(BF16)

    acc = jnp.dot(lhs_ref[...], wbf_ref[...], preferred_element_type=F32)
    acc = acc + b_ref[...]
    if silu:
        acc = _silu(acc)
    if has_resid:
        acc = acc + r_ref[...]
    o_ref[...] = acc.astype(o_ref.dtype)


def _matmul(lhs, w3, b2, out_dtype, *, n_tiles, tm=MM_TM, tn=MM_TN, w_tile0=0,
            resid=None, silu=False, name):
    m, k = lhs.shape
    ni = m // tm
    w_mode = pl.Buffered(1) if tn > MM_TN else None
    in_specs = [
        pl.BlockSpec((tm, k), lambda j, i: (i, 0)),
        pl.BlockSpec((None, k, tn), lambda j, i: (0, 0, j + w_tile0), pipeline_mode=w_mode),
        pl.BlockSpec((1, tn), lambda j, i: (0, j + w_tile0)),
    ]
    args = [lhs, w3, b2]
    if resid is not None:
        in_specs.append(pl.BlockSpec((tm, tn), lambda j, i: (i, j)))
        args.append(resid)
    return pl.pallas_call(
        functools.partial(_matmul_kernel, has_resid=resid is not None, silu=silu),
        grid=(n_tiles, ni),
        in_specs=in_specs,
        out_specs=pl.BlockSpec((tm, tn), lambda j, i: (i, j)),
        out_shape=jax.ShapeDtypeStruct((m, n_tiles * tn), out_dtype),
        scratch_shapes=[pltpu.VMEM((k, tn), BF16)],
        compiler_params=_params("arbitrary", "arbitrary"),
        name=name,
    )(*args)


def _attn_kernel(sink_ref, q_ref, gate_ref, k_ref, v_ref, kp_ref, vp_ref, o_ref,
                 s_ref, p_ref, vbd_ref, *, blocks_per_seq):
    n = pl.program_id(0)
    width = 2 * BLOCK
    dot_w = HEADS_PER_DOT * HEAD_DIM
    halves = GROUP // HEADS_PER_DOT
    n_dots = N_KV_HEADS * halves

    lane128 = lax.broadcasted_iota(jnp.int32, (width, LANES), 1)
    bd_lane_group = lax.broadcasted_iota(jnp.int32, (width, dot_w), 1) // HEAD_DIM

    def block_diag(pair_f32, odd, scale):
        rolled = pltpu.roll(pair_f32, HEAD_DIM, axis=1)
        if odd:
            rep = jnp.where(lane128 < HEAD_DIM, rolled, pair_f32)
        else:
            rep = jnp.where(lane128 < HEAD_DIM, pair_f32, rolled)
        rep = (rep * scale).astype(BF16)
        rep = jnp.concatenate([rep, rep], axis=1)
        zero = jnp.zeros_like(rep)
        return jnp.concatenate(
            [jnp.where(bd_lane_group == h, rep, zero) for h in range(HEADS_PER_DOT)],
            axis=0)

    def q_cols(t):
        head0 = t * HEADS_PER_DOT
        return slice(head0 * HEAD_DIM, head0 * HEAD_DIM + dot_w)

    scale = HEAD_DIM ** -0.5
    for pair in range(N_KV_HEADS // 2):
        lanes = slice(pair * LANES, (pair + 1) * LANES)
        k_pair = jnp.concatenate([kp_ref[:, lanes], k_ref[:, lanes]], axis=0).astype(F32)
        v_pair = jnp.concatenate([vp_ref[:, lanes], v_ref[:, lanes]], axis=0).astype(F32)
        for odd in range(2):
            kvh = 2 * pair + odd
            kbd_t = block_diag(k_pair, odd, scale)
            vbd_ref[kvh] = block_diag(v_pair, odd, 1.0)
            for half in range(halves):
                t = kvh * halves + half
                s_ref[t] = lax.dot_general(q_ref[:, q_cols(t)], kbd_t,
                                           (((1,), (1,)), ((), ())),
                                           preferred_element_type=F32)

    rows = n_dots * BLOCK
    r_in_block = lax.broadcasted_iota(jnp.int32, (rows, BLOCK), 0) & (BLOCK - 1)
    col = lax.broadcasted_iota(jnp.int32, (rows, BLOCK), 1)
    take_prev = col > r_in_block
    no_prev = jnp.where((n % blocks_per_seq) == 0, NEG_BIG, 0.0).astype(F32)
    fold_bias = jnp.where(take_prev, no_prev, 0.0)
    zero_p = jnp.zeros((rows, BLOCK), BF16)
    for h in range(HEADS_PER_DOT):
        lo = h * width
        s_prev = s_ref[:, :, lo:lo + BLOCK].reshape(rows, BLOCK)
        s_cur = s_ref[:, :, lo + BLOCK:lo + width].reshape(rows, BLOCK)
        sink = jnp.broadcast_to(sink_ref[h][:, None, :],
                                (n_dots, BLOCK, BLOCK)).reshape(rows, BLOCK)
        sf = jnp.where(take_prev, s_prev, s_cur) + fold_bias
        m = jnp.maximum(jnp.max(sf, axis=-1, keepdims=True), sink)
        e = jnp.exp(sf - m)
        l = jnp.sum(e, axis=-1, keepdims=True) + jnp.exp(sink - m)
        p = (e / l).astype(BF16)
        p_ref[:, :, lo:lo + BLOCK] = (
            jnp.where(take_prev, p, zero_p).reshape(n_dots, BLOCK, BLOCK))
        p_ref[:, :, lo + BLOCK:lo + width] = (
            jnp.where(take_prev, zero_p, p).reshape(n_dots, BLOCK, BLOCK))

    for t in range(n_dots):
        o = jnp.dot(p_ref[t], vbd_ref[t // halves], preferred_element_type=F32)
        cols = q_cols(t)
        o_ref[:, cols] = (o * gate_ref[:, cols].astype(F32)).astype(o_ref.dtype)


def _attention(qkv, gate, sinks, seq):
    m = qkv.shape[0]
    nblk = m // BLOCK
    bps = seq // BLOCK
    kcol = ATTN_INNER // KV_DIM
    prev = lambda n: jnp.where(n % bps == 0, n, n - 1)
    n_dots = N_Q_HEADS // HEADS_PER_DOT
    keys = HEADS_PER_DOT * 2 * BLOCK
    sink_tab = jnp.broadcast_to(sinks.reshape(n_dots, HEADS_PER_DOT).T[:, :, None],
                                (HEADS_PER_DOT, n_dots, BLOCK))
    return pl.pallas_call(
        functools.partial(_attn_kernel, blocks_per_seq=bps),
        grid=(nblk,),
        in_specs=[
            pl.BlockSpec((HEADS_PER_DOT, n_dots, BLOCK), lambda n: (0, 0, 0)),
            pl.BlockSpec((BLOCK, ATTN_INNER), lambda n: (n, 0)),
            pl.BlockSpec((BLOCK, ATTN_INNER), lambda n: (n, 0)),
            pl.BlockSpec((BLOCK, KV_DIM), lambda n: (n, kcol)),
            pl.BlockSpec((BLOCK, KV_DIM), lambda n: (n, kcol + 1)),
            pl.BlockSpec((BLOCK, KV_DIM), lambda n: (prev(n), kcol)),
            pl.BlockSpec((BLOCK, KV_DIM), lambda n: (prev(n), kcol + 1)),
        ],
        out_specs=pl.BlockSpec((BLOCK, ATTN_INNER), lambda n: (n, 0)),
        out_shape=jax.ShapeDtypeStruct((m, ATTN_INNER), BF16),
        scratch_shapes=[pltpu.VMEM((n_dots, BLOCK, keys), F32),
                        pltpu.VMEM((n_dots, BLOCK, keys), BF16),
                        pltpu.VMEM((N_KV_HEADS, keys, HEADS_PER_DOT * HEAD_DIM), BF16)],
        compiler_params=_params("parallel"),
        name="swa_attention",
    )(sink_tab, qkv, gate, qkv, qkv, qkv, qkv)


def _shift_up_one_row(x, keep_mask):
    r = pltpu.roll(x, SUBLANES - 1, axis=1)
    nxt = jnp.concatenate([r[1:], r[-1:]], axis=0)
    return jnp.where(keep_mask, r, nxt)


def _causal_conv_group(u_ref, wb_ref, row0, lanes, keep_mask):
    nv = CONV_ROW_GROUP // SUBLANES
    off0 = CONV_HALO - (CONV_WIDTH - 1)

    def window(row, n):
        return u_ref[pl.ds(row, n * SUBLANES), lanes].reshape(n, SUBLANES, LANES)

    def phase_sum(s, n):
        acc = None
        for q in range((off0 + CONV_WIDTH - 1) // SUBLANES + 1):
            k = q * SUBLANES + s - off0
            if 0 <= k < CONV_WIDTH:
                term = wb_ref[k, :, lanes][None] * window(row0 + q * SUBLANES, n)
                acc = term if acc is None else acc + term
        return acc

    partial = phase_sum(SUBLANES - 1, nv + 1)
    for s in range(SUBLANES - 2, 0, -1):
        partial = phase_sum(s, nv + 1) + _shift_up_one_row(partial, keep_mask)
    out = phase_sum(0, nv) + _shift_up_one_row(partial, keep_mask)[:nv]
    return out.reshape(CONV_ROW_GROUP, LANES)


def _glu_conv_kernel(h_ref, wa_ref, wg_ref, wz_ref, ba_ref, bg_ref, bz_ref, dw_ref, dwb_ref,
                     c_ref, gate_ref, wa_bf, wg_bf, wz_bf, wb_ref, halo_ref, *u_refs,
                     tiles_per_seq):
    i = pl.program_id(1)
    tm, tn = c_ref.shape

    @pl.when(i == 0)
    def _():
        wa_bf[...] = wa_ref[...].astype(BF16)
        wg_bf[...] = wg_ref[...].astype(BF16)
        wz_bf[...] = wz_ref[...].astype(BF16)
        for k in range(CONV_WIDTH):
            wb_ref[k] = jnp.broadcast_to(dw_ref[k:k + 1, :], (SUBLANES, tn))

    @pl.when(i % tiles_per_seq == 0)
    def _():
        halo_ref[...] = jnp.zeros((CONV_HALO, tn), F32)

    keep_mask = lax.broadcasted_iota(
        jnp.int32, (CONV_ROW_GROUP // SUBLANES + 1, SUBLANES, LANES), 1) < SUBLANES - 1

    n_chunks = len(u_refs)
    kdim = h_ref.shape[1]

    def dot_steps(c, w_bfs, finish):
        rows = slice(c * CONV_CHUNK, (c + 1) * CONV_CHUNK)
        acc = [None] * len(w_bfs)

        def step(n, k0):
            ks = slice(k0, k0 + GLU_K_SLICE)
            part = jnp.dot(h_ref[rows, ks], w_bfs[n][ks, :], preferred_element_type=F32)
            acc[n] = part if acc[n] is None else acc[n] + part

        steps = []
        for k0 in range(0, kdim, GLU_K_SLICE):
            for n in range(len(w_bfs)):
                steps.append(functools.partial(step, n, k0))
        steps.append(lambda: finish(rows, *acc))
        return steps

    def glu_steps(c):
        def finish(rows, a, g):
            u_refs[c][CONV_HALO:, :] = (a + ba_ref[...]) * _sigmoid(g + bg_ref[...])
        return dot_steps(c, (wa_bf, wg_bf), finish)

    def gate_steps(c):
        def finish(rows, z):
            gate_ref[rows, :] = _silu(z + bz_ref[...]).astype(gate_ref.dtype)
        return dot_steps(c, (wz_bf,), finish)

    def fill_halo(c):
        src = halo_ref[...] if c == 0 else u_refs[c - 1][CONV_CHUNK:, :]
        u_refs[c][:CONV_HALO, :] = src

    def conv_steps(c):
        steps = []
        for lc in range(tn // LANES):
            lanes = slice(lc * LANES, (lc + 1) * LANES)
            for r0 in range(0, CONV_CHUNK, CONV_ROW_GROUP):
                def run(lanes=lanes, r0=r0):
                    out = _causal_conv_group(u_refs[c], wb_ref, r0, lanes, keep_mask)
                    o0 = c * CONV_CHUNK + r0
                    c_ref[o0:o0 + CONV_ROW_GROUP, lanes] = out + dwb_ref[:, lanes]
                steps.append(run)
        return steps

    def interleave(mxu_steps, vpu_steps):
        n = len(vpu_steps)
        for t in range(n):
            for f in mxu_steps[t * len(mxu_steps) // n:(t + 1) * len(mxu_steps) // n]:
                f()
            vpu_steps[t]()

    for f in glu_steps(0):
        f()
    for c in range(n_chunks):
        fill_halo(c)
        if c + 1 < n_chunks:
            interleave(glu_steps(c + 1), conv_steps(c))
        else:
            gate = [f for r in range(n_chunks) for f in gate_steps(r)]
            interleave(gate, conv_steps(c))

    halo_ref[...] = u_refs[-1][CONV_CHUNK:, :]


def _glu_conv(h, w3, b2, dw_w, dw_b, seq):
    m, k = h.shape
    c = CONV_INNER
    tn = GLU_TN
    nj, ni = c // tn, m // MM_TM
    w_spec = lambda part: pl.BlockSpec((None, k, tn), lambda j, i: (0, 0, j + part * nj))
    b_spec = lambda part: pl.BlockSpec((1, tn), lambda j, i: (0, j + part * nj))
    out_spec = pl.BlockSpec((MM_TM, tn), lambda j, i: (i, j))
    return pl.pallas_call(
        functools.partial(_glu_conv_kernel, tiles_per_seq=seq // MM_TM),
        grid=(nj, ni),
        in_specs=[
            pl.BlockSpec((MM_TM, k), lambda j, i: (i, 0)),
            w_spec(0), w_spec(1), w_spec(2),
            b_spec(0), b_spec(1), b_spec(2),
            pl.BlockSpec((CONV_WIDTH, tn), lambda j, i: (0, j)),
            pl.BlockSpec((1, tn), lambda j, i: (0, j)),
        ],
        out_specs=[out_spec, out_spec],
        out_shape=[jax.ShapeDtypeStruct((m, c), F32), jax.ShapeDtypeStruct((m, c), BF16)],
        scratch_shapes=[pltpu.VMEM((k, tn), BF16)] * 3
                       + [pltpu.VMEM((CONV_WIDTH, SUBLANES, tn), F32),
                          pltpu.VMEM((CONV_HALO, tn), F32)]
                       + [pltpu.VMEM((CONV_HALO + CONV_CHUNK, tn), F32)] * (MM_TM // CONV_CHUNK),
        compiler_params=_params("arbitrary", "arbitrary"),
        name="glu_conv_gate_in_proj",
    )(h, w3, w3, w3, b2, b2, b2, dw_w, dw_b.reshape(1, c))


def _ln_gate_kernel(c_ref, gate_ref, lng_ref, lnb_ref, o_ref):
    def group(r, carry):
        rows = pl.ds(pl.multiple_of(r * LN_ROW_GROUP, LN_ROW_GROUP), LN_ROW_GROUP)
        c = c_ref[rows, :]
        mu = jnp.mean(c, axis=-1, keepdims=True)
        d = c - mu
        rstd = lax.rsqrt(jnp.mean(d * d, axis=-1, keepdims=True) + EPS)
        for l0 in range(0, c.shape[1], LN_LANE_CHUNK):
            lanes = slice(l0, l0 + LN_LANE_CHUNK)
            y = d[:, lanes] * rstd * lng_ref[:, lanes] + lnb_ref[:, lanes]
            o_ref[rows, lanes] = (_silu(y) * gate_ref[rows, lanes].astype(F32)).astype(o_ref.dtype)
        return carry

    lax.fori_loop(0, o_ref.shape[0] // LN_ROW_GROUP, group, 0, unroll=LN_UNROLL)


def _ln_gate(c, gate, ln_g, ln_b):
    m, d = c.shape
    row = lambda i: (i, 0)
    const = lambda i: (0, 0)
    return pl.pallas_call(
        _ln_gate_kernel,
        grid=(m // NORM_ROWS,),
        in_specs=[pl.BlockSpec((NORM_ROWS, d), row),
                  pl.BlockSpec((NORM_ROWS, d), row),
                  pl.BlockSpec((1, d), const), pl.BlockSpec((1, d), const)],
        out_specs=pl.BlockSpec((NORM_ROWS, d), row),
        out_shape=jax.ShapeDtypeStruct((m, d), BF16),
        compiler_params=_params("parallel"),
        name="ln_gate",
    )(c, gate, ln_g.reshape(1, d), ln_b.reshape(1, d))


OUT_TM = 512
OUT_TN = 1024


def kernel(x, norm_g, attn_w_in, attn_b_in, attn_sinks, attn_w_out, attn_b_out,
           conv_w_in, conv_b_in, conv_dw_w, conv_dw_b, conv_ln_g, conv_ln_b,
           conv_w_out, conv_b_out, final_g):
    batch, seq, d = x.shape
    m = batch * seq
    x0 = x.reshape(m, d)

    h = _rmsnorm(x0, norm_g[0], BF16)
    qkv_tiles = (ATTN_INNER + 2 * KV_DIM) // MM_TN
    qkv = _matmul(h, attn_w_in, attn_b_in, BF16, n_tiles=qkv_tiles, name="attn_qkv_proj")
    gate = _matmul(h, attn_w_in, attn_b_in, BF16, n_tiles=ATTN_INNER // MM_TN,
                   w_tile0=qkv_tiles, silu=True, name="attn_gate_proj")
    y = _attention(qkv, gate, attn_sinks[0], seq)
    x1 = _matmul(y, attn_w_out, attn_b_out, F32, n_tiles=d // OUT_TN, tm=OUT_TM, tn=OUT_TN,
                 resid=x0, name="attn_out_proj")

    h = _rmsnorm(x1, norm_g[1], BF16)
    c, gate = _glu_conv(h, conv_w_in, conv_b_in, conv_dw_w[0], conv_dw_b[0], seq)
    y = _ln_gate(c, gate, conv_ln_g[0], conv_ln_b[0])
    x2 = _matmul(y, conv_w_out, conv_b_out, F32, n_tiles=d // OUT_TN, tm=OUT_TM, tn=OUT_TN,
                 resid=x1, name="conv_out_proj")

    return _rmsnorm(x2, final_g, F32).reshape(batch, seq, d)
```
